```python
import math
import jax, jax.numpy as jnp
from jax import lax
import numpy as np

D_MODEL = 1024
BATCH = 8
SEQ = 2048
DEPTH = 2

N_A = max(1, DEPTH // 2)
N_B = DEPTH - N_A
D_CONV = D_MODEL
CONV_K = 3
N_HEADS = 16
HEAD_DIM = D_MODEL // N_HEADS
D_ATTN = N_HEADS * HEAD_DIM
Q_BLOCK = 128
EPS = 1e-6

kernel_name = "yoco_shortconv_stickbreaking_adaln"


def rmsnorm(x, g):
    xf = x.astype(jnp.float32)
    y = xf * lax.rsqrt(jnp.mean(xf * xf, axis=-1, keepdims=True) + EPS)
    return (y * g.astype(jnp.float32)).astype(x.dtype)


def modulation(c, w, b, n):
    m = jax.nn.silu(c) @ w + b
    return jnp.split(m, n, axis=-1)


def modulated_norm(x, g, shift, scale):
    return rmsnorm(x, g) * (1.0 + scale[:, None, :]) + shift[:, None, :]


def causal_dwconv(u, w):
    return lax.conv_general_dilated(
        u, w[:, None, :].astype(u.dtype), window_strides=(1,),
        padding=[(CONV_K - 1, 0)], dimension_numbers=("NWC", "WIO", "NWC"),
        feature_group_count=u.shape[-1])


def short_conv_layer(x, c, mod_w, mod_b, norm_g, w_in, conv_w, w_out):
    shift, scale, gate = modulation(c, mod_w, mod_b, 3)
    h = modulated_norm(x, norm_g, shift, scale)
    proj = h @ w_in
    b_gate, c_gate, u, z = jnp.split(proj, 4, axis=-1)
    y = b_gate * causal_dwconv(c_gate * u, conv_w)
    y = y * jax.nn.silu(z)
    return x + gate[:, None, :] * (y @ w_out)


def stick_breaking_attention(q, k, v):
    seq = q.shape[2]
    inv_sqrt_d = 1.0 / math.sqrt(q.shape[-1])
    outs = []
    for i in range(seq // Q_BLOCK):
        q0 = i * Q_BLOCK
        kend = q0 + Q_BLOCK
        qb = q[:, :, q0:kend].astype(jnp.float32)
        kb = k[:, :, :kend].astype(jnp.float32)
        vb = v[:, :, :kend]
        z = jnp.einsum('bhqd,bhkd->bhqk', qb, kb) * inv_sqrt_d
        qpos = q0 + jnp.arange(Q_BLOCK)[:, None]
        kpos = jnp.arange(kend)[None, :]
        mask = kpos < qpos
        log_beta = jax.nn.log_sigmoid(z)
        log_1m_beta = jnp.where(mask, log_beta - z, 0.0)
        suffix = lax.cumsum(log_1m_beta, axis=3, reverse=True) - log_1m_beta
        a = jnp.where(mask, jnp.exp(log_beta + suffix), 0.0)
        outs.append(jnp.einsum('bhqk,bhkd->bhqd', a.astype(vb.dtype), vb))
    return jnp.concatenate(outs, axis=2)


def split_heads(t):
    b, s, _ = t.shape
    return t.reshape(b, s, N_HEADS, HEAD_DIM).transpose(0, 2, 1, 3)


def stick_breaking_layer(x, c, k, v, mod_w, mod_b, norm_g, w_qz, w_out):
    shift, scale, gate = modulation(c, mod_w, mod_b, 3)
    h = modulated_norm(x, norm_g, shift, scale)
    proj = h @ w_qz
    q, z = jnp.split(proj, 2, axis=-1)
    o = stick_breaking_attention(split_heads(q), k, v)
    b, _, s, _ = o.shape
    o = o.transpose(0, 2, 1, 3).reshape(b, s, D_ATTN)
    o = o * jax.nn.silu(z)
    return x + gate[:, None, :] * (o @ w_out)


def setup_inputs(seed: int = 0) -> dict:
    key = jax.random.key(seed)
    ks = jax.random.split(key, 20)
    D = D_MODEL
    nrm = lambda k, shape, s: jax.random.normal(k, shape, jnp.float32) * s
    return {
        "x": nrm(ks[0], (BATCH, SEQ, D), 1.0),
        "c": nrm(ks[1], (BATCH, D), 1.0),
        "a_mod_w": nrm(ks[2], (N_A, D, 3 * D), 0.02),
        "a_mod_b": nrm(ks[3], (N_A, 3 * D), 0.01),
        "a_norm_g": 1.0 + nrm(ks[4], (N_A, D), 0.02),
        "a_w_in": nrm(ks[5], (N_A, D, 4 * D_CONV), D ** -0.5),
        "a_conv_w": nrm(ks[6], (N_A, CONV_K, D_CONV), CONV_K ** -0.5),
        "a_w_out": nrm(ks[7], (N_A, D_CONV, D), D_CONV ** -0.5),
        "kv_mod_w": nrm(ks[8], (D, 2 * D), 0.02),
        "kv_mod_b": nrm(ks[9], (2 * D,), 0.01),
        "kv_norm_g": 1.0 + nrm(ks[10], (D,), 0.02),
        "w_kv": nrm(ks[11], (D, 2 * D_ATTN), D ** -0.5),
        "b_mod_w": nrm(ks[12], (N_B, D, 3 * D), 0.02),
        "b_mod_b": nrm(ks[13], (N_B, 3 * D), 0.01),
        "b_norm_g": 1.0 + nrm(ks[14], (N_B, D), 0.02),
        "b_w_qz": nrm(ks[15], (N_B, D, 2 * D_ATTN), D ** -0.5),
        "b_w_out": nrm(ks[16], (N_B, D_ATTN, D), D_ATTN ** -0.5),
        "final_norm_g": 1.0 + nrm(ks[17], (D,), 0.02),
    }


def reference(x, c, a_mod_w, a_mod_b, a_norm_g, a_w_in, a_conv_w, a_w_out,
              kv_mod_w, kv_mod_b, kv_norm_g, w_kv,
              b_mod_w, b_mod_b, b_norm_g, b_w_qz, b_w_out, final_norm_g):
    k = v = None
    for layer in range(DEPTH):
        if layer < N_A:
            x = short_conv_layer(x, c, a_mod_w[layer], a_mod_b[layer], a_norm_g[layer],
                                 a_w_in[layer], a_conv_w[layer], a_w_out[layer])
        else:
            if layer == N_A:
                kv_shift, kv_scale = modulation(c, kv_mod_w, kv_mod_b, 2)
                hkv = modulated_norm(x, kv_norm_g, kv_shift, kv_scale)
                k_flat, v_flat = jnp.split(hkv @ w_kv, 2, axis=-1)
                k, v = split_heads(k_flat), split_heads(v_flat)
            j = layer - N_A
            x = stick_breaking_layer(x, c, k, v, b_mod_w[j], b_mod_b[j], b_norm_g[j],
                                     b_w_qz[j], b_w_out[j])
    return rmsnorm(x, final_norm_g)
```

```python
import functools
import math

import jax
import jax.numpy as jnp
from jax import lax
from jax.experimental import pallas as pl
from jax.experimental.pallas import tpu as pltpu

D_MODEL = 1024
N_HEADS = 16
HEAD_DIM = D_MODEL // N_HEADS
CONV_K = 3
EPS = 1e-6
LOG2E = 1.4426950408889634

LANES = 128
SUBLANES = 8
MXU_DIM = 256
VMEM_BYTES = 64 * 1024 * 1024

HEADS_PER_GROUP = LANES // HEAD_DIM
N_GROUPS = N_HEADS // HEADS_PER_GROUP
K_BLOCK = MXU_DIM
Q_TILE = MXU_DIM

BF16 = jnp.bfloat16
F32 = jnp.float32


def _dot(a, b):
    return jnp.dot(a, b, preferred_element_type=F32)


def _silu(x):
    return x * jax.nn.sigmoid(x)


def _mod_kernel(c_ref, w_ref, b_ref, o_ref):
    s = _silu(c_ref[...]).astype(BF16)
    o_ref[...] = _dot(s, w_ref[...].astype(BF16)) + b_ref[...]


def _modulation(c, w, b, *, tn=512):
    bsz, d = c.shape
    n = w.shape[1]
    return pl.pallas_call(
        _mod_kernel,
        grid=(n // tn,),
        in_specs=[
            pl.BlockSpec((bsz, d), lambda j: (0, 0)),
            pl.BlockSpec((d, tn), lambda j: (0, j)),
            pl.BlockSpec((1, tn), lambda j: (0, j)),
        ],
        out_specs=pl.BlockSpec((bsz, tn), lambda j: (0, j)),
        out_shape=jax.ShapeDtypeStruct((bsz, n), F32),
        name="modulation",
    )(c, w, b.reshape(1, n))


def _conv_layer_kernel(x_ref, mod_ref, g_ref, win_ref, cw_ref, wout_ref, o_ref, carry_ref):
    tm = x_ref.shape[1]
    d = x_ref.shape[2]

    @pl.when(pl.program_id(1) == 0)
    def _():
        carry_ref[...] = jnp.zeros_like(carry_ref)

    x = x_ref[0]
    mod = mod_ref[0]
    shift, scale, gate = mod[0:1], mod[1:2], mod[2:3]
    rs = lax.rsqrt(jnp.mean(x * x, axis=-1, keepdims=True) + EPS)
    h = ((x * rs) * (g_ref[...] * (1.0 + scale)) + shift).astype(BF16)

    b_gate = _dot(h, win_ref[:, 0 * d:1 * d])
    c_gate = _dot(h, win_ref[:, 1 * d:2 * d])
    u = _dot(h, win_ref[:, 2 * d:3 * d])
    z = _dot(h, win_ref[:, 3 * d:4 * d])

    cu = c_gate * u
    ext = jnp.concatenate([carry_ref[...], cu], axis=0)
    prev1 = pltpu.roll(ext, 1, 0)[SUBLANES:]
    prev2 = pltpu.roll(ext, 2, 0)[SUBLANES:]
    carry_ref[...] = cu[tm - SUBLANES:]
    cw = cw_ref[...]
    conv = cw[0:1] * prev2 + cw[1:2] * prev1 + cw[2:3] * cu

    y = (b_gate * conv) * _silu(z)
    o_ref[0] = x + gate * _dot(y.astype(BF16), wout_ref[...])


def _conv_layer(x, mod, g, w_in, conv_w, w_out, *, tm=512):
    bsz, s, d = x.shape
    const = dict(pipeline_mode=pl.Buffered(1))
    return pl.pallas_call(
        _conv_layer_kernel,
        grid=(bsz, s // tm),
        in_specs=[
            pl.BlockSpec((1, tm, d), lambda b, i: (b, i, 0)),
            pl.BlockSpec((1, 3, d), lambda b, i: (b, 0, 0)),
            pl.BlockSpec((1, d), lambda b, i: (0, 0)),
            pl.BlockSpec((d, 4 * d), lambda b, i: (0, 0), **const),
            pl.BlockSpec((CONV_K, d), lambda b, i: (0, 0)),
            pl.BlockSpec((d, d), lambda b, i: (0, 0), **const),
        ],
        out_specs=pl.BlockSpec((1, tm, d), lambda b, i: (b, i, 0)),
        out_shape=jax.ShapeDtypeStruct((bsz, s, d), F32),
        scratch_shapes=[pltpu.VMEM((SUBLANES, d), F32)],
        compiler_params=pltpu.CompilerParams(
            dimension_semantics=("arbitrary", "arbitrary"),
            vmem_limit_bytes=48 * 1024 * 1024),
        name="conv_layer",
    )(x, mod, g, w_in, conv_w, w_out)


def _proj_kernel(x_ref, modkv_ref, modb_ref, gkv_ref, gb_ref, wkv_ref, wqz_ref,
                 q_ref, kv_ref, z_ref):
    d = x_ref.shape[2]
    x = x_ref[0]
    xn = x * lax.rsqrt(jnp.mean(x * x, axis=-1, keepdims=True) + EPS)
    modkv = modkv_ref[0]
    modb = modb_ref[0]
    hkv = (xn * (gkv_ref[...] * (1.0 + modkv[1:2])) + modkv[0:1]).astype(BF16)
    hq = (xn * (gb_ref[...] * (1.0 + modb[1:2])) + modb[0:1]).astype(BF16)
    kv_ref[0] = _dot(hkv, wkv_ref[...]).astype(BF16)
    q_ref[0] = (_dot(hq, wqz_ref[:, :d]) * (LOG2E / math.sqrt(HEAD_DIM))).astype(BF16)
    z_ref[0] = _dot(hq, wqz_ref[:, d:])


def _projections(x, modkv, modb, gkv, gb, w_kv, w_qz, *, tm=512):
    bsz, s, d = x.shape
    const = dict(pipeline_mode=pl.Buffered(1))
    row = lambda b, i: (b, i, 0)
    return pl.pallas_call(
        _proj_kernel,
        grid=(bsz, s // tm),
        in_specs=[
            pl.BlockSpec((1, tm, d), row),
            pl.BlockSpec((1, 2, d), lambda b, i: (b, 0, 0)),
            pl.BlockSpec((1, 3, d), lambda b, i: (b, 0, 0)),
            pl.BlockSpec((1, d), lambda b, i: (0, 0)),
            pl.BlockSpec((1, d), lambda b, i: (0, 0)),
            pl.BlockSpec((d, 2 * d), lambda b, i: (0, 0), **const),
            pl.BlockSpec((d, 2 * d), lambda b, i: (0, 0), **const),
        ],
        out_specs=[
            pl.BlockSpec((1, tm, d), row),
            pl.BlockSpec((1, tm, 2 * d), row),
            pl.BlockSpec((1, tm, d), row),
        ],
        out_shape=[
            jax.ShapeDtypeStruct((bsz, s, d), BF16),
            jax.ShapeDtypeStruct((bsz, s, 2 * d), BF16),
            jax.ShapeDtypeStruct((bsz, s, d), F32),
        ],
        compiler_params=pltpu.CompilerParams(
            dimension_semantics=("arbitrary", "arbitrary"),
            vmem_limit_bytes=48 * 1024 * 1024),
        name="projections",
    )(x, modkv, modb, gkv, gb, w_kv, w_qz)


def _suffix_weights():
    j = lax.broadcasted_iota(jnp.int32, (MXU_DIM, MXU_DIM), 0) % LANES
    s = lax.broadcasted_iota(jnp.int32, (MXU_DIM, MXU_DIM), 1)
    return jnp.where((s >= LANES) | (j >= s), -1.0, 0.0).astype(BF16)


def _attn_kernel(q_ref, k_ref, v_ref, z_ref, w_ref, o_ref, acc_ref, c0_ref, c1_ref):
    n_q = q_ref.shape[1] // Q_TILE
    lane = lax.broadcasted_iota(jnp.int32, (1, LANES), 1)
    head0 = lane < HEAD_DIM
    row = lax.broadcasted_iota(jnp.int32, (Q_TILE, LANES), 0)
    col = lax.broadcasted_iota(jnp.int32, (Q_TILE, LANES), 1)
    diag_mask = (col < row, col + LANES < row)
    w = w_ref[...]

    def head_block(qh, k_blk, carry, diag):
        z2 = lax.dot_general(qh, k_blk, (((1,), (1,)), ((), ())), preferred_element_type=F32)
        a = []
        suffix = carry
        for c in (1, 0):
            zc = z2[:, c * LANES:(c + 1) * LANES]
            sp = jnp.maximum(zc, 0.0) + jnp.log(1.0 + jnp.exp2(-jnp.abs(zc))) * LOG2E
            if diag:
                sp = jnp.where(diag_mask[c], sp, 0.0)
            hi = sp.astype(BF16)
            lo = (sp - hi.astype(F32)).astype(BF16)
            r = _dot(jnp.concatenate([hi, lo], axis=1), w)
            ac = jnp.exp2(zc + (r[:, :LANES] + suffix))
            if diag:
                ac = jnp.where(diag_mask[c], ac, 0.0)
            a.append(ac.astype(BF16))
            suffix = suffix + r[:, LANES:]
        return jnp.concatenate([a[1], a[0]], axis=1), suffix

    def key_block(q0, q1, kb, diag):
        start = pl.multiple_of(kb * K_BLOCK, K_BLOCK)
        k_blk = k_ref[0, pl.ds(start, K_BLOCK), :]
        v_blk = v_ref[0, pl.ds(start, K_BLOCK), :]
        a0, c0 = head_block(q0, k_blk, c0_ref[...], diag)
        a1, c1 = head_block(q1, k_blk, c1_ref[...], diag)
        c0_ref[...] = c0
        c1_ref[...] = c1
        zero = jnp.zeros_like(v_blk)
        v2 = jnp.concatenate([jnp.where(head0, v_blk, zero), jnp.where(head0, zero, v_blk)], axis=0)
        acc_ref[...] += _dot(jnp.concatenate([a0, a1], axis=1), v2)

    def q_tile(qi, _):
        qstart = pl.multiple_of(qi * Q_TILE, Q_TILE)
        q = q_ref[0, pl.ds(qstart, Q_TILE), :]
        zero = jnp.zeros_like(q)
        q0 = jnp.where(head0, q, zero)
        q1 = jnp.where(head0, zero, q)
        acc_ref[...] = jnp.zeros_like(acc_ref)
        c0_ref[...] = jnp.zeros_like(c0_ref)
        c1_ref[...] = jnp.zeros_like(c1_ref)
        key_block(q0, q1, qi, True)

        def below_diagonal(j, _):
            key_block(q0, q1, qi - 1 - j, False)
            return 0

        lax.fori_loop(0, qi, below_diagonal, 0)
        zg = z_ref[0, pl.ds(qstart, Q_TILE), :]
        o_ref[0, pl.ds(qstart, Q_TILE), :] = (acc_ref[...] * _silu(zg)).astype(o_ref.dtype)
        return 0

    lax.fori_loop(0, n_q, q_tile, 0)


def _attention(q, kv, z):
    bsz, s, d = q.shape
    grp = lambda b, g: (b, 0, g)
    return pl.pallas_call(
        _attn_kernel,
        grid=(bsz, N_GROUPS),
        in_specs=[
            pl.BlockSpec((1, s, LANES), grp),
            pl.BlockSpec((1, s, LANES), grp),
            pl.BlockSpec((1, s, LANES), lambda b, g: (b, 0, N_GROUPS + g)),
            pl.BlockSpec((1, s, LANES), grp),
            pl.BlockSpec((MXU_DIM, MXU_DIM), lambda b, g: (0, 0)),
        ],
        out_specs=pl.BlockSpec((1, s, LANES), grp),
        out_shape=jax.ShapeDtypeStruct((bsz, s, d), BF16),
        scratch_shapes=[pltpu.VMEM((Q_TILE, LANES), F32)] * 3,
        compiler_params=pltpu.CompilerParams(
            dimension_semantics=("arbitrary", "arbitrary"),
            vmem_limit_bytes=32 * 1024 * 1024),
        name="stickbreak_attention",
    )(q, kv, kv, z, _suffix_weights())


def _out_kernel(o_ref, x_ref, mod_ref, wout_ref, g_ref, y_ref):
    gate = mod_ref[0][2:3]
    x = x_ref[0] + gate * _dot(o_ref[0], wout_ref[...])
    y_ref[0] = (x * lax.rsqrt(jnp.mean(x * x, axis=-1, keepdims=True) + EPS)) * g_ref[...]


def _out_layer(o, x, modb, w_out, g, *, tm=512):
    bsz, s, d = x.shape
    row = lambda b, i: (b, i, 0)
    return pl.pallas_call(
        _out_kernel,
        grid=(bsz, s // tm),
        in_specs=[
            pl.BlockSpec((1, tm, d), row),
            pl.BlockSpec((1, tm, d), row),
            pl.BlockSpec((1, 3, d), lambda b, i: (b, 0, 0)),
            pl.BlockSpec((d, d), lambda b, i: (0, 0)),
            pl.BlockSpec((1, d), lambda b, i: (0, 0)),
        ],
        out_specs=pl.BlockSpec((1, tm, d), row),
        out_shape=jax.ShapeDtypeStruct((bsz, s, d), F32),
        compiler_params=pltpu.CompilerParams(
            dimension_semantics=("arbitrary", "arbitrary"),
            vmem_limit_bytes=32 * 1024 * 1024),
        name="out_layer",
    )(o, x, modb, w_out, g)


def kernel(x, c, a_mod_w, a_mod_b, a_norm_g, a_w_in, a_conv_w, a_w_out,
           kv_mod_w, kv_mod_b, kv_norm_g, w_kv,
           b_mod_w, b_mod_b, b_norm_g, b_w_qz, b_w_out, final_norm_g):
    bsz, _, d = x.shape
    assert d == D_MODEL and a_mod_w.shape[0] == 1 and b_mod_w.shape[0] == 1

    mod_a = _modulation(c, a_mod_w[0], a_mod_b[0]).reshape(bsz, 3, d)
    mod_kv = _modulation(c, kv_mod_w, kv_mod_b).reshape(bsz, 2, d)
    mod_b = _modulation(c, b_mod_w[0], b_mod_b[0]).reshape(bsz, 3, d)

    x1 = _conv_layer(x, mod_a, a_norm_g, a_w_in[0].astype(BF16), a_conv_w[0],
                     a_w_out[0].astype(BF16))
    q, kv, z = _projections(x1, mod_kv, mod_b, kv_norm_g.reshape(1, d), b_norm_g,
                            w_kv.astype(BF16), b_w_qz[0].astype(BF16))
    o = _attention(q, kv, z)
    return _out_layer(o, x1, mod_b, b_w_out[0].astype(BF16), final_norm_g.reshape(1, d))
```

```python
import math

import jax
import jax.numpy as jnp
from jax import lax
from jax.experimental import pallas as pl
from jax.experimental.pallas import tpu as pltpu

D_MODEL = 1024
N_HEADS = 16
HEAD_DIM = D_MODEL // N_HEADS
CONV_K = 3
EPS = 1e-6
LOG2E = 1.4426950408889634

LANES = 128
SUBLANES = 8
MXU_DIM = 256

K_BLOCK = MXU_DIM
Q_TILE = K_BLOCK

BF16 = jnp.bfloat16
F32 = jnp.float32


def _dot(a, b):
    return jnp.dot(a, b, preferred_element_type=F32)


def _silu(x):
    return x * jax.nn.sigmoid(x)


def _mod_kernel(c_ref, w_ref, b_ref, o_ref):
    s = _silu(c_ref[...]).astype(BF16)
    o_ref[...] = _dot(s, w_ref[...].astype(BF16)) + b_ref[...]


def _modulation(c, w, b, *, tn=512):
    bsz, d = c.shape
    n = w.shape[1]
    return pl.pallas_call(
        _mod_kernel,
        grid=(n // tn,),
        in_specs=[
            pl.BlockSpec((bsz, d), lambda j: (0, 0)),
            pl.BlockSpec((d, tn), lambda j: (0, j)),
            pl.BlockSpec((1, tn), lambda j: (0, j)),
        ],
        out_specs=pl.BlockSpec((bsz, tn), lambda j: (0, j)),
        out_shape=jax.ShapeDtypeStruct((bsz, n), F32),
        name="modulation",
    )(c, w, b.reshape(1, n))


def _conv_layer_kernel(x_ref, mod_ref, g_ref, win_ref, cw_ref, wout_ref, o_ref, carry_ref):
    tm = x_ref.shape[1]
    d = x_ref.shape[2]

    @pl.when(pl.program_id(1) == 0)
    def _():
        carry_ref[...] = jnp.zeros_like(carry_ref)

    x = x_ref[0]
    mod = mod_ref[0]
    shift, scale, gate = mod[0:1], mod[1:2], mod[2:3]
    rs = lax.rsqrt(jnp.mean(x * x, axis=-1, keepdims=True) + EPS)
    h = ((x * rs) * (g_ref[...] * (1.0 + scale)) + shift).astype(BF16)

    b_gate = _dot(h, win_ref[:, 0 * d:1 * d])
    c_gate = _dot(h, win_ref[:, 1 * d:2 * d])
    u = _dot(h, win_ref[:, 2 * d:3 * d])
    z = _dot(h, win_ref[:, 3 * d:4 * d])

    cu = c_gate * u
    ext = jnp.concatenate([carry_ref[...], cu], axis=0)
    prev1 = pltpu.roll(ext, 1, 0)[SUBLANES:]
    prev2 = pltpu.roll(ext, 2, 0)[SUBLANES:]
    carry_ref[...] = cu[tm - SUBLANES:]
    cw = cw_ref[...]
    conv = cw[0:1] * prev2 + cw[1:2] * prev1 + cw[2:3] * cu

    y = (b_gate * conv) * _silu(z)
    o_ref[0] = x + gate * _dot(y.astype(BF16), wout_ref[...])


def _conv_layer(x, mod, g, w_in, conv_w, w_out, *, tm=512):
    bsz, s, d = x.shape
    const = dict(pipeline_mode=pl.Buffered(1))
    return pl.pallas_call(
        _conv_layer_kernel,
        grid=(bsz, s // tm),
        in_specs=[
            pl.BlockSpec((1, tm, d), lambda b, i: (b, i, 0)),
            pl.BlockSpec((1, 3, d), lambda b, i: (b, 0, 0)),
            pl.BlockSpec((1, d), lambda b, i: (0, 0)),
            pl.BlockSpec((d, 4 * d), lambda b, i: (0, 0), **const),
            pl.BlockSpec((CONV_K, d), lambda b, i: (0, 0)),
            pl.BlockSpec((d, d), lambda b, i: (0, 0), **const),
        ],
        out_specs=pl.BlockSpec((1, tm, d), lambda b, i: (b, i, 0)),
        out_shape=jax.ShapeDtypeStruct((bsz, s, d), F32),
        scratch_shapes=[pltpu.VMEM((SUBLANES, d), F32)],
        compiler_params=pltpu.CompilerParams(
            dimension_semantics=("arbitrary", "arbitrary"),
            vmem_limit_bytes=48 * 1024 * 1024),
        name="conv_layer",
    )(x, mod, g, w_in, conv_w, w_out)


def _proj_kernel(x_ref, modkv_ref, modb_ref, gkv_ref, gb_ref, wkv_ref, wqz_ref,
                 q_ref, kv_ref, z_ref):
    d = x_ref.shape[2]
    x = x_ref[0]
    xn = x * lax.rsqrt(jnp.mean(x * x, axis=-1, keepdims=True) + EPS)
    modkv = modkv_ref[0]
    modb = modb_ref[0]
    hkv = (xn * (gkv_ref[...] * (1.0 + modkv[1:2])) + modkv[0:1]).astype(BF16)
    hq = (xn * (gb_ref[...] * (1.0 + modb[1:2])) + modb[0:1]).astype(BF16)
    kv_ref[0] = _dot(hkv, wkv_ref[...]).astype(BF16)
    q_ref[0] = (_dot(hq, wqz_ref[:, :d]) * (LOG2E / math.sqrt(HEAD_DIM))).astype(BF16)
    z_ref[0] = _dot(hq, wqz_ref[:, d:])


def _projections(x, modkv, modb, gkv, gb, w_kv, w_qz, *, tm=512):
    bsz, s, d = x.shape
    const = dict(pipeline_mode=pl.Buffered(1))
    row = lambda b, i: (b, i, 0)
    return pl.pallas_call(
        _proj_kernel,
        grid=(bsz, s // tm),
        in_specs=[
            pl.BlockSpec((1, tm, d), row),
            pl.BlockSpec((1, 2, d), lambda b, i: (b, 0, 0)),
            pl.BlockSpec((1, 3, d), lambda b, i: (b, 0, 0)),
            pl.BlockSpec((1, d), lambda b, i: (0, 0)),
            pl.BlockSpec((1, d), lambda b, i: (0, 0)),
            pl.BlockSpec((d, 2 * d), lambda b, i: (0, 0), **const),
            pl.BlockSpec((d, 2 * d), lambda b, i: (0, 0), **const),
        ],
        out_specs=[
            pl.BlockSpec((1, tm, d), row),
            pl.BlockSpec((1, tm, 2 * d), row),
            pl.BlockSpec((1, tm, d), row),
        ],
        out_shape=[
            jax.ShapeDtypeStruct((bsz, s, d), BF16),
            jax.ShapeDtypeStruct((bsz, s, 2 * d), BF16),
            jax.ShapeDtypeStruct((bsz, s, d), F32),
        ],
        compiler_params=pltpu.CompilerParams(
            dimension_semantics=("arbitrary", "arbitrary"),
            vmem_limit_bytes=48 * 1024 * 1024),
        name="projections",
    )(x, modkv, modb, gkv, gb, w_kv, w_qz)


def _suffix_weights():
    j = lax.broadcasted_iota(jnp.int32, (K_BLOCK, K_BLOCK), 0)
    s = lax.broadcasted_iota(jnp.int32, (K_BLOCK, K_BLOCK), 1)
    return jnp.where(j >= s, -1.0, 0.0).astype(BF16)


def _attn_kernel(q_ref, k_ref, v_ref, z_ref, w_ref, o_ref,
                 qm_ref, acc_ref, carry_ref, z2_ref, a_ref):
    n_q = q_ref.shape[1] // Q_TILE
    n_groups = q_ref.shape[2] // LANES
    heads = range(2 * n_groups)
    lane = lax.broadcasted_iota(jnp.int32, (1, LANES), 1)
    head0 = lane < HEAD_DIM
    row = lax.broadcasted_iota(jnp.int32, (Q_TILE, K_BLOCK), 0)
    col = lax.broadcasted_iota(jnp.int32, (Q_TILE, K_BLOCK), 1)
    causal = col < row
    w = w_ref[...]

    def block_start(kb):
        return pl.multiple_of(kb * K_BLOCK, K_BLOCK)

    def logits(kb):
        out = []
        for h in heads:
            g = h // 2
            k_blk = k_ref[0, pl.ds(block_start(kb), K_BLOCK), g * LANES:(g + 1) * LANES]
            out.append(lax.dot_general(qm_ref[h], k_blk, (((1,), (1,)), ((), ())),
                                       preferred_element_type=F32))
        return out

    def weights(z2, masked):
        r = []
        for h in heads:
            sp = jnp.maximum(z2[h], 0.0) + jnp.log(1.0 + jnp.exp2(-jnp.abs(z2[h]))) * LOG2E
            if masked:
                sp = jnp.where(causal, sp, 0.0)
            r.append(_dot(sp.astype(BF16), w))
        a = []
        for h in heads:
            carry = carry_ref[h]
            ah = jnp.exp2(z2[h] + (r[h] + jnp.concatenate([carry] * (K_BLOCK // LANES), axis=1)))
            if masked:
                ah = jnp.where(causal, ah, 0.0)
            a.append(ah.astype(BF16))
            carry_ref[h] = carry + r[h][:, 0:1]
        return a

    def accumulate(a, kb):
        for h in heads:
            g = h // 2
            v_blk = v_ref[0, pl.ds(block_start(kb), K_BLOCK), g * LANES:(g + 1) * LANES]
            acc_ref[h] += _dot(a[h], v_blk)

    def q_tile(qi, _):
        qstart = pl.multiple_of(qi * Q_TILE, Q_TILE)
        for g in range(n_groups):
            q = q_ref[0, pl.ds(qstart, Q_TILE), g * LANES:(g + 1) * LANES]
            zero = jnp.zeros_like(q)
            qm_ref[2 * g] = jnp.where(head0, q, zero)
            qm_ref[2 * g + 1] = jnp.where(head0, zero, q)
        acc_ref[...] = jnp.zeros_like(acc_ref)
        carry_ref[...] = jnp.zeros_like(carry_ref)

        z2 = logits(qi)
        z2_next = logits(jnp.maximum(qi - 1, 0))
        a = weights(z2, masked=True)
        for h in heads:
            z2_ref[h] = z2_next[h]
            a_ref[h] = a[h]

        def below_diagonal(j, _):
            kb = qi - 1 - j
            z2 = [z2_ref[h] for h in heads]
            a_prev = [a_ref[h] for h in heads]
            z2_next = logits(jnp.maximum(kb - 1, 0))
            accumulate(a_prev, kb + 1)
            a = weights(z2, masked=False)
            for h in heads:
                z2_ref[h] = z2_next[h]
                a_ref[h] = a[h]
            return 0

        lax.fori_loop(0, qi, below_diagonal, 0)
        accumulate([a_ref[h] for h in heads], 0)
        for g in range(n_groups):
            lanes = slice(g * LANES, (g + 1) * LANES)
            zg = z_ref[0, pl.ds(qstart, Q_TILE), lanes]
            acc = jnp.where(head0, acc_ref[2 * g], acc_ref[2 * g + 1])
            o_ref[0, pl.ds(qstart, Q_TILE), lanes] = (acc * _silu(zg)).astype(o_ref.dtype)
        return 0

    lax.fori_loop(0, n_q, q_tile, 0)


def _attention(q, kv, z, *, groups_per_step=4):
    bsz, s, d = q.shape
    width = groups_per_step * LANES
    n_steps = d // width
    n_heads = 2 * groups_per_step
    grp = lambda b, g: (b, 0, g)
    return pl.pallas_call(
        _attn_kernel,
        grid=(bsz, n_steps),
        in_specs=[
            pl.BlockSpec((1, s, width), grp),
            pl.BlockSpec((1, s, width), grp),
            pl.BlockSpec((1, s, width), lambda b, g: (b, 0, n_steps + g)),
            pl.BlockSpec((1, s, width), grp),
            pl.BlockSpec((K_BLOCK, K_BLOCK), lambda b, g: (0, 0)),
        ],
        out_specs=pl.BlockSpec((1, s, width), grp),
        out_shape=jax.ShapeDtypeStruct((bsz, s, d), BF16),
        scratch_shapes=[
            pltpu.VMEM((n_heads, Q_TILE, LANES), BF16),
            pltpu.VMEM((n_heads, Q_TILE, LANES), F32),
            pltpu.VMEM((n_heads, Q_TILE, LANES), F32),
            pltpu.VMEM((n_heads, Q_TILE, K_BLOCK), F32),
            pltpu.VMEM((n_heads, Q_TILE, K_BLOCK), BF16),
        ],
        compiler_params=pltpu.CompilerParams(
            dimension_semantics=("arbitrary", "arbitrary"),
            vmem_limit_bytes=40 * 1024 * 1024),
        name="stickbreak_attention",
    )(q, kv, kv, z, _suffix_weights())


def _out_kernel(o_ref, x_ref, mod_ref, wout_ref, g_ref, y_ref):
    gate = mod_ref[0][2:3]
    x = x_ref[0] + gate * _dot(o_ref[0], wout_ref[...])
    y_ref[0] = (x * lax.rsqrt(jnp.mean(x * x, axis=-1, keepdims=True) + EPS)) * g_ref[...]


def _out_layer(o, x, modb, w_out, g, *, tm=512):
    bsz, s, d = x.shape
    row = lambda b, i: (b, i, 0)
    return pl.pallas_call(
        _out_kernel,
        grid=(bsz, s // tm),
        in_specs=[
            pl.BlockSpec((1, tm, d), row),
            pl.BlockSpec((1, tm, d), row),
            pl.BlockSpec((1, 3, d), lambda b, i: (b, 0, 0)),
            pl.BlockSpec((d, d), lambda b, i: (0, 0)),
            pl.BlockSpec((1, d), lambda b, i: (0, 0)),
        ],
        out_specs=pl.BlockSpec((1, tm, d), row),
        out_shape=jax.ShapeDtypeStruct((bsz, s, d), F32),
        compiler_params=pltpu.CompilerParams(
            dimension_semantics=("arbitrary", "arbitrary"),
            vmem_limit_bytes=32 * 1024 * 1024),
        name="out_layer",
    )(o, x, modb, w_out, g)


def kernel(x, c, a_mod_w, a_mod_b, a_norm_g, a_w_in, a_conv_w, a_w_out,
           kv_mod_w, kv_mod_b, kv_norm_g, w_kv,
           b_mod_w, b_mod_b, b_norm_g, b_w_qz, b_w_out, final_norm_g):
    bsz, _, d = x.shape
    assert d == D_MODEL and a_mod_w.shape[0] == 1 and b_mod_w.shape[0] == 1

    mod_a = _modulation(c, a_mod_w[0], a_mod_b[0]).reshape(bsz, 3, d)
    mod_kv = _modulation(c, kv_mod_w, kv_mod_b).reshape(bsz, 2, d)
    mod_b = _modulation(c, b_mod_w[0], b_mod_b[0]).reshape(bsz, 3, d)

    x1 = _conv_layer(x, mod_a, a_norm_g, a_w_in[0].astype(BF16), a_conv_w[0],
                     a_w_out[0].astype(BF16))
    q, kv, z = _projections(x1, mod_kv, mod_b, kv_norm_g.reshape(1, d), b_norm_g,
                            w_kv.astype(BF16), b_w_qz[0].astype(BF16))
    o = _attention(q, kv, z)
    return _out_layer(o, x1, mod_b, b_w_out[0].astype(BF16), final_norm_g.reshape(1, d))
```

```python
import math

import jax
import jax.numpy as jnp
from jax import lax
from jax.experimental import pallas as pl
from jax.experimental.pallas import tpu as pltpu

D_MODEL = 1024
N_HEADS = 16
HEAD_DIM = D_MODEL // N_HEADS
CONV_K = 3
EPS = 1e-6
LOG2E = 1.4426950408889634

LANES = 128
SUBLANES = 8
MXU_DIM = 256

K_BLOCK = MXU_DIM
Q_TILE = K_BLOCK
NEGLIGIBLE_LOG2 = -160.0

BF16 = jnp.bfloat16
F32 = jnp.float32


def _dot(a, b):
    return jnp.dot(a, b, preferred_element_type=F32)


def _silu(x):
    return x * jax.nn.sigmoid(x)


def _mod_kernel(c_ref, w_ref, b_ref, o_ref):
    s = _silu(c_ref[...]).astype(BF16)
    o_ref[...] = _dot(s, w_ref[...].astype(BF16)) + b_ref[...]


def _modulation(c, w, b, *, tn=512):
    bsz, d = c.shape
    n = w.shape[1]
    return pl.pallas_call(
        _mod_kernel,
        grid=(n // tn,),
        in_specs=[
            pl.BlockSpec((bsz, d), lambda j: (0, 0)),
            pl.BlockSpec((d, tn), lambda j: (0, j)),
            pl.BlockSpec((1, tn), lambda j: (0, j)),
        ],
        out_specs=pl.BlockSpec((bsz, tn), lambda j: (0, j)),
        out_shape=jax.ShapeDtypeStruct((bsz, n), F32),
        name="modulation",
    )(c, w, b.reshape(1, n))


def _conv_layer_kernel(x_ref, mod_ref, g_ref, win_ref, cw_ref, wout_ref, o_ref, carry_ref):
    tm = x_ref.shape[1]
    d = x_ref.shape[2]

    @pl.when(pl.program_id(1) == 0)
    def _():
        carry_ref[...] = jnp.zeros_like(carry_ref)

    x = x_ref[0]
    mod = mod_ref[0]
    shift, scale, gate = mod[0:1], mod[1:2], mod[2:3]
    rs = lax.rsqrt(jnp.mean(x * x, axis=-1, keepdims=True) + EPS)
    h = ((x * rs) * (g_ref[...] * (1.0 + scale)) + shift).astype(BF16)

    b_gate = _dot(h, win_ref[:, 0 * d:1 * d])
    c_gate = _dot(h, win_ref[:, 1 * d:2 * d])
    u = _dot(h, win_ref[:, 2 * d:3 * d])
    z = _dot(h, win_ref[:, 3 * d:4 * d])

    cu = c_gate * u
    ext = jnp.concatenate([carry_ref[...], cu], axis=0)
    prev1 = pltpu.roll(ext, 1, 0)[SUBLANES:]
    prev2 = pltpu.roll(ext, 2, 0)[SUBLANES:]
    carry_ref[...] = cu[tm - SUBLANES:]
    cw = cw_ref[...]
    conv = cw[0:1] * prev2 + cw[1:2] * prev1 + cw[2:3] * cu

    y = (b_gate * conv) * _silu(z)
    o_ref[0] = x + gate * _dot(y.astype(BF16), wout_ref[...])


def _conv_layer(x, mod, g, w_in, conv_w, w_out, *, tm=512):
    bsz, s, d = x.shape
    const = dict(pipeline_mode=pl.Buffered(1))
    return pl.pallas_call(
        _conv_layer_kernel,
        grid=(bsz, s // tm),
        in_specs=[
            pl.BlockSpec((1, tm, d), lambda b, i: (b, i, 0)),
            pl.BlockSpec((1, 3, d), lambda b, i: (b, 0, 0)),
            pl.BlockSpec((1, d), lambda b, i: (0, 0)),
            pl.BlockSpec((d, 4 * d), lambda b, i: (0, 0), **const),
            pl.BlockSpec((CONV_K, d), lambda b, i: (0, 0)),
            pl.BlockSpec((d, d), lambda b, i: (0, 0), **const),
        ],
        out_specs=pl.BlockSpec((1, tm, d), lambda b, i: (b, i, 0)),
        out_shape=jax.ShapeDtypeStruct((bsz, s, d), F32),
        scratch_shapes=[pltpu.VMEM((SUBLANES, d), F32)],
        compiler_params=pltpu.CompilerParams(
            dimension_semantics=("arbitrary", "arbitrary"),
            vmem_limit_bytes=48 * 1024 * 1024),
        name="conv_layer",
    )(x, mod, g, w_in, conv_w, w_out)


def _proj_kernel(x_ref, modkv_ref, modb_ref, gkv_ref, gb_ref, wkv_ref, wqz_ref,
                 q_ref, kv_ref, z_ref):
    d = x_ref.shape[2]
    x = x_ref[0]
    xn = x * lax.rsqrt(jnp.mean(x * x, axis=-1, keepdims=True) + EPS)
    modkv = modkv_ref[0]
    modb = modb_ref[0]
    hkv = (xn * (gkv_ref[...] * (1.0 + modkv[1:2])) + modkv[0:1]).astype(BF16)
    hq = (xn * (gb_ref[...] * (1.0 + modb[1:2])) + modb[0:1]).astype(BF16)
    kv_ref[0] = _dot(hkv, wkv_ref[...]).astype(BF16)
    q_ref[0] = (_dot(hq, wqz_ref[:, :d]) * (LOG2E / math.sqrt(HEAD_DIM))).astype(BF16)
    z_ref[0] = _dot(hq, wqz_ref[:, d:])


def _projections(x, modkv, modb, gkv, gb, w_kv, w_qz, *, tm=512):
    bsz, s, d = x.shape
    const = dict(pipeline_mode=pl.Buffered(1))
    row = lambda b, i: (b, i, 0)
    return pl.pallas_call(
        _proj_kernel,
        grid=(bsz, s // tm),
        in_specs=[
            pl.BlockSpec((1, tm, d), row),
            pl.BlockSpec((1, 2, d), lambda b, i: (b, 0, 0)),
            pl.BlockSpec((1, 3, d), lambda b, i: (b, 0, 0)),
            pl.BlockSpec((1, d), lambda b, i: (0, 0)),
            pl.BlockSpec((1, d), lambda b, i: (0, 0)),
            pl.BlockSpec((d, 2 * d), lambda b, i: (0, 0), **const),
            pl.BlockSpec((d, 2 * d), lambda b, i: (0, 0), **const),
        ],
        out_specs=[
            pl.BlockSpec((1, tm, d), row),
            pl.BlockSpec((1, tm, 2 * d), row),
            pl.BlockSpec((1, tm, d), row),
        ],
        out_shape=[
            jax.ShapeDtypeStruct((bsz, s, d), BF16),
            jax.ShapeDtypeStruct((bsz, s, 2 * d), BF16),
            jax.ShapeDtypeStruct((bsz, s, d), F32),
        ],
        compiler_params=pltpu.CompilerParams(
            dimension_semantics=("arbitrary", "arbitrary"),
            vmem_limit_bytes=48 * 1024 * 1024),
        name="projections",
    )(x, modkv, modb, gkv, gb, w_kv, w_qz)


def _suffix_weights():
    j = lax.broadcasted_iota(jnp.int32, (K_BLOCK, K_BLOCK), 0)
    s = lax.broadcasted_iota(jnp.int32, (K_BLOCK, K_BLOCK), 1)
    return jnp.where((j > s) | (s == K_BLOCK - 1), -1.0, 0.0).astype(BF16)


def _attn_kernel(q_ref, k_ref, v_ref, z_ref, w_ref, o_ref,
                 qm_ref, acc_ref, carry_ref, z2_ref, a_ref):
    n_q = q_ref.shape[1] // Q_TILE
    n_groups = q_ref.shape[2] // LANES
    heads = range(2 * n_groups)
    lane = lax.broadcasted_iota(jnp.int32, (1, LANES), 1)
    head0 = lane < HEAD_DIM
    row = lax.broadcasted_iota(jnp.int32, (Q_TILE, K_BLOCK), 0)
    col = lax.broadcasted_iota(jnp.int32, (Q_TILE, K_BLOCK), 1)
    causal = col < row
    last_key = col == K_BLOCK - 1
    w = w_ref[...]

    def block_start(kb):
        return pl.multiple_of(kb * K_BLOCK, K_BLOCK)

    def logits(kb):
        out = []
        for h in heads:
            g = h // 2
            k_blk = k_ref[0, pl.ds(block_start(kb), K_BLOCK), g * LANES:(g + 1) * LANES]
            out.append(lax.dot_general(qm_ref[h], k_blk, (((1,), (1,)), ((), ())),
                                       preferred_element_type=F32))
        return out

    def weights(masked):
        r = []
        for h in heads:
            z2 = z2_ref[h]
            sp = jnp.maximum(z2, 0.0) + jnp.log(1.0 + jnp.exp2(-jnp.abs(z2))) * LOG2E
            z2_ref[h] = z2 - sp
            if masked:
                sp = jnp.where(causal, sp, 0.0)
            r.append(_dot(sp.astype(BF16), w))
        carry = [carry_ref[h] for h in heads]
        worst = None
        for h in heads:
            nxt = carry[h] + r[h][:, K_BLOCK - 1:]
            carry_ref[h] = nxt
            worst = nxt if worst is None else jnp.maximum(worst, nxt)
        worst = jnp.max(worst)
        a = []
        for h in heads:
            suffix = jnp.where(last_key, 0.0, r[h])
            ah = jnp.exp2(z2_ref[h] + (suffix + jnp.concatenate([carry[h]] * (K_BLOCK // LANES), axis=1)))
            if masked:
                ah = jnp.where(causal, ah, 0.0)
            a.append(ah.astype(BF16))
        return a, worst

    def accumulate(a, kb):
        for h in heads:
            g = h // 2
            v_blk = v_ref[0, pl.ds(block_start(kb), K_BLOCK), g * LANES:(g + 1) * LANES]
            acc_ref[h] += _dot(a[h], v_blk)

    def q_tile(qi, _):
        qstart = pl.multiple_of(qi * Q_TILE, Q_TILE)
        for g in range(n_groups):
            q = q_ref[0, pl.ds(qstart, Q_TILE), g * LANES:(g + 1) * LANES]
            zero = jnp.zeros_like(q)
            qm_ref[2 * g] = jnp.where(head0, q, zero)
            qm_ref[2 * g + 1] = jnp.where(head0, zero, q)
        acc_ref[...] = jnp.zeros_like(acc_ref)
        carry_ref[...] = jnp.zeros_like(carry_ref)

        z2 = logits(qi)
        for h in heads:
            z2_ref[h] = z2[h]
        z2_next = logits(jnp.maximum(qi - 1, 0))
        a, worst = weights(masked=True)
        for h in heads:
            z2_ref[h] = z2_next[h]
            a_ref[h] = a[h]

        def more_blocks(state):
            j, worst = state
            return jnp.logical_and(j < qi, worst > NEGLIGIBLE_LOG2)

        def below_diagonal(state):
            j, _ = state
            kb = qi - 1 - j
            a_prev = [a_ref[h] for h in heads]
            z2_next = logits(jnp.maximum(kb - 1, 0))
            accumulate(a_prev, kb + 1)
            a, worst = weights(masked=False)
            for h in heads:
                z2_ref[h] = z2_next[h]
                a_ref[h] = a[h]
            return j + 1, worst

        n_below, _ = lax.while_loop(more_blocks, below_diagonal, (jnp.int32(0), worst))
        accumulate([a_ref[h] for h in heads], qi - n_below)
        for g in range(n_groups):
            lanes = slice(g * LANES, (g + 1) * LANES)
            zg = z_ref[0, pl.ds(qstart, Q_TILE), lanes]
            acc = jnp.where(head0, acc_ref[2 * g], acc_ref[2 * g + 1])
            o_ref[0, pl.ds(qstart, Q_TILE), lanes] = (acc * _silu(zg)).astype(o_ref.dtype)
        return 0

    lax.fori_loop(0, n_q, q_tile, 0)


def _attention(q, kv, z, *, groups_per_step=4):
    bsz, s, d = q.shape
    width = groups_per_step * LANES
    n_steps = d // width
    n_heads = 2 * groups_per_step
    grp = lambda b, g: (b, 0, g)
    return pl.pallas_call(
        _attn_kernel,
        grid=(bsz, n_steps),
        in_specs=[
            pl.BlockSpec((1, s, width), grp),
            pl.BlockSpec((1, s, width), grp),
            pl.BlockSpec((1, s, width), lambda b, g: (b, 0, n_steps + g)),
            pl.BlockSpec((1, s, width), grp),
            pl.BlockSpec((K_BLOCK, K_BLOCK), lambda b, g: (0, 0)),
        ],
        out_specs=pl.BlockSpec((1, s, width), grp),
        out_shape=jax.ShapeDtypeStruct((bsz, s, d), BF16),
        scratch_shapes=[
            pltpu.VMEM((n_heads, Q_TILE, LANES), BF16),
            pltpu.VMEM((n_heads, Q_TILE, LANES), F32),
            pltpu.VMEM((n_heads, Q_TILE, LANES), F32),
            pltpu.VMEM((n_heads, Q_TILE, K_BLOCK), F32),
            pltpu.VMEM((n_heads, Q_TILE, K_BLOCK), BF16),
        ],
        compiler_params=pltpu.CompilerParams(
            dimension_semantics=("arbitrary", "arbitrary"),
            vmem_limit_bytes=40 * 1024 * 1024),
        name="stickbreak_attention",
    )(q, kv, kv, z, _suffix_weights())


def _out_kernel(o_ref, x_ref, mod_ref, wout_ref, g_ref, y_ref):
    gate = mod_ref[0][2:3]
    x = x_ref[0] + gate * _dot(o_ref[0], wout_ref[...])
    y_ref[0] = (x * lax.rsqrt(jnp.mean(x * x, axis=-1, keepdims=True) + EPS)) * g_ref[...]


def _out_layer(o, x, modb, w_out, g, *, tm=512):
    bsz, s, d = x.shape
    row = lambda b, i: (b, i, 0)
    return pl.pallas_call(
        _out_kernel,
        grid=(bsz, s // tm),
        in_specs=[
            pl.BlockSpec((1, tm, d), row),
            pl.BlockSpec((1, tm, d), row),
            pl.BlockSpec((1, 3, d), lambda b, i: (b, 0, 0)),
            pl.BlockSpec((d, d), lambda b, i: (0, 0)),
            pl.BlockSpec((1, d), lambda b, i: (0, 0)),
        ],
        out_specs=pl.BlockSpec((1, tm, d), row),
        out_shape=jax.ShapeDtypeStruct((bsz, s, d), F32),
        compiler_params=pltpu.CompilerParams(
            dimension_semantics=("arbitrary", "arbitrary"),
            vmem_limit_bytes=32 * 1024 * 1024),
        name="out_layer",
    )(o, x, modb, w_out, g)


def kernel(x, c, a_mod_w, a_mod_b, a_norm_g, a_w_in, a_conv_w, a_w_out,
           kv_mod_w, kv_mod_b, kv_norm_g, w_kv,
           b_mod_w, b_mod_b, b_norm_g, b_w_qz, b_w_out, final_norm_g):
    bsz, _, d = x.shape
    assert d == D_MODEL and a_mod_w.shape[0] == 1 and b_mod_w.shape[0] == 1

    mod_a = _modulation(c, a_mod_w[0], a_mod_b[0]).reshape(bsz, 3, d)
    mod_kv = _modulation(c, kv_mod_w, kv_mod_b).reshape(bsz, 2, d)
    mod_b = _modulation(c, b_mod_w[0], b_mod_b[0]).reshape(bsz, 3, d)

    x1 = _conv_layer(x, mod_a, a_norm_g, a_w_in[0].astype(BF16), a_conv_w[0],
                     a_w_out[0].astype(BF16))
    q, kv, z = _projections(x1, mod_kv, mod_b, kv_norm_g.reshape(1, d), b_norm_g,
                            w_kv.astype(BF16), b_w_qz[0].astype(BF16))
    o = _attention(q, kv, z)
    return _out_layer(o, x1, mod_b, b_w_out[0].astype(BF16), final_norm_g.reshape(1, d))
```

```python
import math

import jax
import jax.numpy as jnp
from jax import lax
from jax.experimental import pallas as pl
from jax.experimental.pallas import tpu as pltpu

D_MODEL = 1024
N_HEADS = 16
HEAD_DIM = D_MODEL // N_HEADS
CONV_K = 3
EPS = 1e-6
LOG2E = 1.4426950408889634

LANES = 128
SUBLANES = 8
MXU_DIM = 256

K_BLOCK = MXU_DIM
Q_TILE = K_BLOCK
NEGLIGIBLE_LOG2 = -160.0
WEIGHTS_LAG = 2

BF16 = jnp.bfloat16
F32 = jnp.float32


def _dot(a, b):
    return jnp.dot(a, b, preferred_element_type=F32)


def _silu(x):
    return x * jax.nn.sigmoid(x)


def _mod_kernel(c_ref, w_ref, b_ref, o_ref):
    s = _silu(c_ref[...]).astype(BF16)
    o_ref[...] = _dot(s, w_ref[...].astype(BF16)) + b_ref[...]


def _modulation(c, w, b, *, tn=512):
    bsz, d = c.shape
    n = w.shape[1]
    return pl.pallas_call(
        _mod_kernel,
        grid=(n // tn,),
        in_specs=[
            pl.BlockSpec((bsz, d), lambda j: (0, 0)),
            pl.BlockSpec((d, tn), lambda j: (0, j)),
            pl.BlockSpec((1, tn), lambda j: (0, j)),
        ],
        out_specs=pl.BlockSpec((bsz, tn), lambda j: (0, j)),
        out_shape=jax.ShapeDtypeStruct((bsz, n), F32),
        name="modulation",
    )(c, w, b.reshape(1, n))


def _conv_layer_kernel(x_ref, mod_ref, g_ref, win_ref, cw_ref, wout_ref, o_ref, carry_ref):
    tm = x_ref.shape[1]
    d = x_ref.shape[2]

    @pl.when(pl.program_id(1) == 0)
    def _():
        carry_ref[...] = jnp.zeros_like(carry_ref)

    x = x_ref[0]
    mod = mod_ref[0]
    shift, scale, gate = mod[0:1], mod[1:2], mod[2:3]
    rs = lax.rsqrt(jnp.mean(x * x, axis=-1, keepdims=True) + EPS)
    h = ((x * rs) * (g_ref[...] * (1.0 + scale)) + shift).astype(BF16)

    b_gate = _dot(h, win_ref[:, 0 * d:1 * d])
    c_gate = _dot(h, win_ref[:, 1 * d:2 * d])
    u = _dot(h, win_ref[:, 2 * d:3 * d])
    z = _dot(h, win_ref[:, 3 * d:4 * d])

    cu = c_gate * u
    ext = jnp.concatenate([carry_ref[...], cu], axis=0)
    prev1 = pltpu.roll(ext, 1, 0)[SUBLANES:]
    prev2 = pltpu.roll(ext, 2, 0)[SUBLANES:]
    carry_ref[...] = cu[tm - SUBLANES:]
    cw = cw_ref[...]
    conv = cw[0:1] * prev2 + cw[1:2] * prev1 + cw[2:3] * cu

    y = (b_gate * conv) * _silu(z)
    o_ref[0] = x + gate * _dot(y.astype(BF16), wout_ref[...])


def _conv_layer(x, mod, g, w_in, conv_w, w_out, *, tm=512):
    bsz, s, d = x.shape
    const = dict(pipeline_mode=pl.Buffered(1))
    return pl.pallas_call(
        _conv_layer_kernel,
        grid=(bsz, s // tm),
        in_specs=[
            pl.BlockSpec((1, tm, d), lambda b, i: (b, i, 0)),
            pl.BlockSpec((1, 3, d), lambda b, i: (b, 0, 0)),
            pl.BlockSpec((1, d), lambda b, i: (0, 0)),
            pl.BlockSpec((d, 4 * d), lambda b, i: (0, 0), **const),
            pl.BlockSpec((CONV_K, d), lambda b, i: (0, 0)),
            pl.BlockSpec((d, d), lambda b, i: (0, 0), **const),
        ],
        out_specs=pl.BlockSpec((1, tm, d), lambda b, i: (b, i, 0)),
        out_shape=jax.ShapeDtypeStruct((bsz, s, d), F32),
        scratch_shapes=[pltpu.VMEM((SUBLANES, d), F32)],
        compiler_params=pltpu.CompilerParams(
            dimension_semantics=("arbitrary", "arbitrary"),
            vmem_limit_bytes=48 * 1024 * 1024),
        name="conv_layer",
    )(x, mod, g, w_in, conv_w, w_out)


def _proj_kernel(x_ref, modkv_ref, modb_ref, gkv_ref, gb_ref, wkv_ref, wqz_ref,
                 q_ref, kv_ref, z_ref):
    d = x_ref.shape[2]
    x = x_ref[0]
    xn = x * lax.rsqrt(jnp.mean(x * x, axis=-1, keepdims=True) + EPS)
    modkv = modkv_ref[0]
    modb = modb_ref[0]
    hkv = (xn * (gkv_ref[...] * (1.0 + modkv[1:2])) + modkv[0:1]).astype(BF16)
    hq = (xn * (gb_ref[...] * (1.0 + modb[1:2])) + modb[0:1]).astype(BF16)
    kv_ref[0] = _dot(hkv, wkv_ref[...]).astype(BF16)
    q_ref[0] = (_dot(hq, wqz_ref[:, :d]) * (LOG2E / math.sqrt(HEAD_DIM))).astype(BF16)
    z_ref[0] = _dot(hq, wqz_ref[:, d:])


def _projections(x, modkv, modb, gkv, gb, w_kv, w_qz, *, tm=512):
    bsz, s, d = x.shape
    const = dict(pipeline_mode=pl.Buffered(1))
    row = lambda b, i: (b, i, 0)
    return pl.pallas_call(
        _proj_kernel,
        grid=(bsz, s // tm),
        in_specs=[
            pl.BlockSpec((1, tm, d), row),
            pl.BlockSpec((1, 2, d), lambda b, i: (b, 0, 0)),
            pl.BlockSpec((1, 3, d), lambda b, i: (b, 0, 0)),
            pl.BlockSpec((1, d), lambda b, i: (0, 0)),
            pl.BlockSpec((1, d), lambda b, i: (0, 0)),
            pl.BlockSpec((d, 2 * d), lambda b, i: (0, 0), **const),
            pl.BlockSpec((d, 2 * d), lambda b, i: (0, 0), **const),
        ],
        out_specs=[
            pl.BlockSpec((1, tm, d), row),
            pl.BlockSpec((1, tm, 2 * d), row),
            pl.BlockSpec((1, tm, d), row),
        ],
        out_shape=[
            jax.ShapeDtypeStruct((bsz, s, d), BF16),
            jax.ShapeDtypeStruct((bsz, s, 2 * d), BF16),
            jax.ShapeDtypeStruct((bsz, s, d), F32),
        ],
        compiler_params=pltpu.CompilerParams(
            dimension_semantics=("arbitrary", "arbitrary"),
            vmem_limit_bytes=48 * 1024 * 1024),
        name="projections",
    )(x, modkv, modb, gkv, gb, w_kv, w_qz)


def _suffix_weights():
    j = lax.broadcasted_iota(jnp.int32, (K_BLOCK, K_BLOCK), 0)
    s = lax.broadcasted_iota(jnp.int32, (K_BLOCK, K_BLOCK), 1)
    return jnp.where((j > s) | (s == K_BLOCK - 1), -1.0, 0.0).astype(BF16)


def _attn_kernel(q_ref, k_ref, v_ref, z_ref, w_ref, o_ref,
                 qm_ref, acc_ref, carry_ref, z2a_ref, z2b_ref, aa_ref, ab_ref):
    n_q = q_ref.shape[1] // Q_TILE
    n_groups = q_ref.shape[2] // LANES
    heads = tuple(range(2 * n_groups))
    lane = lax.broadcasted_iota(jnp.int32, (1, LANES), 1)
    head0 = lane < HEAD_DIM
    row = lax.broadcasted_iota(jnp.int32, (Q_TILE, K_BLOCK), 0)
    col = lax.broadcasted_iota(jnp.int32, (Q_TILE, K_BLOCK), 1)
    causal = col < row
    last_key = col == K_BLOCK - 1
    w = w_ref[...]

    def block_start(kb):
        return pl.multiple_of(kb * K_BLOCK, K_BLOCK)

    def logits(h, kb, z2_buf):
        g = h // 2
        k_blk = k_ref[0, pl.ds(block_start(kb), K_BLOCK), g * LANES:(g + 1) * LANES]
        z2_buf[h] = lax.dot_general(qm_ref[h], k_blk, (((1,), (1,)), ((), ())),
                                    preferred_element_type=F32)

    def accumulate(h, a_buf, kb):
        g = h // 2
        v_blk = v_ref[0, pl.ds(block_start(kb), K_BLOCK), g * LANES:(g + 1) * LANES]
        acc_ref[h] += _dot(a_buf[h], v_blk)

    def suffix_sums(h, z2_buf, masked):
        z2 = z2_buf[h]
        sp = jnp.maximum(z2, 0.0) + jnp.log(1.0 + jnp.exp2(-jnp.abs(z2))) * LOG2E
        z2_buf[h] = z2 - sp
        if masked:
            sp = jnp.where(causal, sp, 0.0)
        return _dot(sp.astype(BF16), w)

    def weights(h, r, z2_buf, a_buf, masked):
        carry = carry_ref[h]
        nxt = carry + r[:, K_BLOCK - 1:]
        carry_ref[h] = nxt
        suffix = jnp.where(last_key, 0.0, r)
        ah = jnp.exp2(z2_buf[h] + (suffix + jnp.concatenate([carry] * (K_BLOCK // LANES), axis=1)))
        if masked:
            ah = jnp.where(causal, ah, 0.0)
        a_buf[h] = ah.astype(BF16)
        return nxt

    def block_step(kb, z2_cur, z2_next, a_cur, a_prev, masked):
        r = {}
        worst = None
        for step in range(len(heads) + WEIGHTS_LAG):
            if step < len(heads):
                h = heads[step]
                logits(h, jnp.maximum(kb - 1, 0), z2_next)
                if a_prev is not None:
                    accumulate(h, a_prev, kb + 1)
                r[h] = suffix_sums(h, z2_cur, masked)
            if step >= WEIGHTS_LAG:
                h = heads[step - WEIGHTS_LAG]
                nxt = weights(h, r.pop(h), z2_cur, a_cur, masked)
                worst = nxt if worst is None else jnp.maximum(worst, nxt)
        return jnp.max(worst)

    def q_tile(qi, _):
        qstart = pl.multiple_of(qi * Q_TILE, Q_TILE)
        for g in range(n_groups):
            q = q_ref[0, pl.ds(qstart, Q_TILE), g * LANES:(g + 1) * LANES]
            zero = jnp.zeros_like(q)
            qm_ref[2 * g] = jnp.where(head0, q, zero)
            qm_ref[2 * g + 1] = jnp.where(head0, zero, q)
        acc_ref[...] = jnp.zeros_like(acc_ref)
        carry_ref[...] = jnp.zeros_like(carry_ref)

        for h in heads:
            logits(h, qi, z2a_ref)
        worst = block_step(qi, z2a_ref, z2b_ref, aa_ref, None, True)

        def more_blocks(state):
            j, worst = state
            return jnp.logical_and(j < qi, worst > NEGLIGIBLE_LOG2)

        def below_diagonal(state):
            j, _ = state
            kb = qi - 1 - j
            worst = lax.cond(
                j % 2 == 0,
                lambda: block_step(kb, z2b_ref, z2a_ref, ab_ref, aa_ref, False),
                lambda: block_step(kb, z2a_ref, z2b_ref, aa_ref, ab_ref, False))
            return j + 1, worst

        n_below, _ = lax.while_loop(more_blocks, below_diagonal, (jnp.int32(0), worst))
        last = qi - n_below

        def drain(a_buf):
            for h in heads:
                accumulate(h, a_buf, last)

        lax.cond(n_below % 2 == 0, lambda: drain(aa_ref), lambda: drain(ab_ref))
        for g in range(n_groups):
            lanes = slice(g * LANES, (g + 1) * LANES)
            zg = z_ref[0, pl.ds(qstart, Q_TILE), lanes]
            acc = jnp.where(head0, acc_ref[2 * g], acc_ref[2 * g + 1])
            o_ref[0, pl.ds(qstart, Q_TILE), lanes] = (acc * _silu(zg)).astype(o_ref.dtype)
        return 0

    lax.fori_loop(0, n_q, q_tile, 0)


def _attention(q, kv, z, *, groups_per_step=4):
    bsz, s, d = q.shape
    width = groups_per_step * LANES
    n_steps = d // width
    n_heads = 2 * groups_per_step
    grp = lambda b, g: (b, 0, g)
    return pl.pallas_call(
        _attn_kernel,
        grid=(bsz, n_steps),
        in_specs=[
            pl.BlockSpec((1, s, width), grp),
            pl.BlockSpec((1, s, width), grp),
            pl.BlockSpec((1, s, width), lambda b, g: (b, 0, n_steps + g)),
            pl.BlockSpec((1, s, width), grp),
            pl.BlockSpec((K_BLOCK, K_BLOCK), lambda b, g: (0, 0)),
        ],
        out_specs=pl.BlockSpec((1, s, width), grp),
        out_shape=jax.ShapeDtypeStruct((bsz, s, d), BF16),
        scratch_shapes=[
            pltpu.VMEM((n_heads, Q_TILE, LANES), BF16),
            pltpu.VMEM((n_heads, Q_TILE, LANES), F32),
            pltpu.VMEM((n_heads, Q_TILE, LANES), F32),
            pltpu.VMEM((n_heads, Q_TILE, K_BLOCK), F32),
            pltpu.VMEM((n_heads, Q_TILE, K_BLOCK), F32),
            pltpu.VMEM((n_heads, Q_TILE, K_BLOCK), BF16),
            pltpu.VMEM((n_heads, Q_TILE, K_BLOCK), BF16),
        ],
        compiler_params=pltpu.CompilerParams(
            dimension_semantics=("arbitrary", "arbitrary"),
            vmem_limit_bytes=40 * 1024 * 1024),
        name="stickbreak_attention",
    )(q, kv, kv, z, _suffix_weights())


def _out_kernel(o_ref, x_ref, mod_ref, wout_ref, g_ref, y_ref):
    gate = mod_ref[0][2:3]
    x = x_ref[0] + gate * _dot(o_ref[0], wout_ref[...])
    y_ref[0] = (x * lax.rsqrt(jnp.mean(x * x, axis=-1, keepdims=True) + EPS)) * g_ref[...]


def _out_layer(o, x, modb, w_out, g, *, tm=512):
    bsz, s, d = x.shape
    row = lambda b, i: (b, i, 0)
    return pl.pallas_call(
        _out_kernel,
        grid=(bsz, s // tm),
        in_specs=[
            pl.BlockSpec((1, tm, d), row),
            pl.BlockSpec((1, tm, d), row),
            pl.BlockSpec((1, 3, d), lambda b, i: (b, 0, 0)),
            pl.BlockSpec((d, d), lambda b, i: (0, 0)),
            pl.BlockSpec((1, d), lambda b, i: (0, 0)),
        ],
        out_specs=pl.BlockSpec((1, tm, d), row),
        out_shape=jax.ShapeDtypeStruct((bsz, s, d), F32),
        compiler_params=pltpu.CompilerParams(
            dimension_semantics=("arbitrary", "arbitrary"),
            vmem_limit_bytes=32 * 1024 * 1024),
        name="out_layer",
    )(o, x, modb, w_out, g)


def kernel(x, c, a_mod_w, a_mod_b, a_norm_g, a_w_in, a_conv_w, a_w_out,
           kv_mod_w, kv_mod_b, kv_norm_g, w_kv,
           b_mod_w, b_mod_b, b_norm_g, b_w_qz, b_w_out, final_norm_g):
    bsz, _, d = x.shape
    assert d == D_MODEL and a_mod_w.shape[0] == 1 and b_mod_w.shape[0] == 1

    mod_a = _modulation(c, a_mod_w[0], a_mod_b[0]).reshape(bsz, 3, d)
    mod_kv = _modulation(c, kv_mod_w, kv_mod_b).reshape(bsz, 2, d)
    mod_b = _modulation(c, b_mod_w[0], b_mod_b[0]).reshape(bsz, 3, d)

    x1 = _conv_layer(x, mod_a, a_norm_g, a_w_in[0].astype(BF16), a_conv_w[0],
                     a_w_out[0].astype(BF16))
    q, kv, z = _projections(x1, mod_kv, mod_b, kv_norm_g.reshape(1, d), b_norm_g,
                            w_kv.astype(BF16), b_w_qz[0].astype(BF16))
    o = _attention(q, kv, z)
    return _out_layer(o, x1, mod_b, b_w_out[0].astype(BF16), final_norm_g.reshape(1, d))
```

```python
import functools
import math

import jax
import jax.numpy as jnp
from jax import lax
from jax.experimental import pallas as pl
from jax.experimental.pallas import tpu as pltpu

D_MODEL = 1024
N_HEADS = 16
HEAD_DIM = D_MODEL // N_HEADS
CONV_K = 3
EPS = 1e-6
LOG2E = 1.4426950408889634

LANES = 128
SUBLANES = 8
MXU_DIM = 256

K_BLOCK = MXU_DIM
Q_TILE = K_BLOCK
NEGLIGIBLE_LOG2 = -160.0
WEIGHTS_LAG = 2

BF16 = jnp.bfloat16
F32 = jnp.float32


def _dot(a, b):
    return jnp.dot(a, b, preferred_element_type=F32)


def _silu(x):
    return x * jax.nn.sigmoid(x)


def _mod_kernel(c_ref, *refs, starts):
    n = len(starts) - 1
    w_refs, b_refs, o_ref = refs[:n], refs[n:2 * n], refs[2 * n]
    j = pl.program_id(0)
    s = _silu(c_ref[...]).astype(BF16)
    for i in range(n):
        @pl.when(jnp.logical_and(j >= starts[i], j < starts[i + 1]))
        def _(i=i):
            o_ref[...] = _dot(s, w_refs[i][...].astype(BF16)) + b_refs[i][...]


def _modulation(c, ws, bs, *, tn=1024):
    bsz, d = c.shape
    tiles = [w.shape[1] // tn for w in ws]
    starts = [sum(tiles[:i]) for i in range(len(ws) + 1)]

    def tile_of(i):
        return lambda j: (0, jnp.clip(j - starts[i], 0, tiles[i] - 1))

    return pl.pallas_call(
        functools.partial(_mod_kernel, starts=tuple(starts)),
        grid=(starts[-1],),
        in_specs=([pl.BlockSpec((bsz, d), lambda j: (0, 0))]
                  + [pl.BlockSpec((d, tn), tile_of(i)) for i in range(len(ws))]
                  + [pl.BlockSpec((1, tn), tile_of(i)) for i in range(len(ws))]),
        out_specs=pl.BlockSpec((bsz, tn), lambda j: (0, j)),
        out_shape=jax.ShapeDtypeStruct((bsz, starts[-1] * tn), F32),
        compiler_params=pltpu.CompilerParams(
            dimension_semantics=("arbitrary",),
            vmem_limit_bytes=40 * 1024 * 1024),
        name="modulation",
    )(c, *ws, *[b.reshape(1, -1) for b in bs])


def _conv_layer_kernel(x_ref, mod_ref, g_ref, win_ref, cw_ref, wout_ref, o_ref, carry_ref):
    tm = x_ref.shape[1]
    d = x_ref.shape[2]

    @pl.when(pl.program_id(1) == 0)
    def _():
        carry_ref[...] = jnp.zeros_like(carry_ref)

    x = x_ref[0]
    mod = mod_ref[0]
    shift, scale, gate = mod[0:1], mod[1:2], mod[2:3]
    rs = lax.rsqrt(jnp.mean(x * x, axis=-1, keepdims=True) + EPS)
    h = ((x * rs) * (g_ref[...] * (1.0 + scale)) + shift).astype(BF16)

    b_gate = _dot(h, win_ref[:, 0 * d:1 * d])
    c_gate = _dot(h, win_ref[:, 1 * d:2 * d])
    u = _dot(h, win_ref[:, 2 * d:3 * d])
    z = _dot(h, win_ref[:, 3 * d:4 * d])

    cu = c_gate * u
    ext = jnp.concatenate([carry_ref[...], cu], axis=0)
    prev1 = pltpu.roll(ext, 1, 0)[SUBLANES:]
    prev2 = pltpu.roll(ext, 2, 0)[SUBLANES:]
    carry_ref[...] = cu[tm - SUBLANES:]
    cw = cw_ref[...]
    conv = cw[0:1] * prev2 + cw[1:2] * prev1 + cw[2:3] * cu

    y = (b_gate * conv) * _silu(z)
    o_ref[0] = x + gate * _dot(y.astype(BF16), wout_ref[...])


def _conv_layer(x, mod, g, w_in, conv_w, w_out, *, tm=1024):
    bsz, s, d = x.shape
    const = dict(pipeline_mode=pl.Buffered(1))
    return pl.pallas_call(
        _conv_layer_kernel,
        grid=(bsz, s // tm),
        in_specs=[
            pl.BlockSpec((1, tm, d), lambda b, i: (b, i, 0)),
            pl.BlockSpec((1, 3, d), lambda b, i: (b, 0, 0)),
            pl.BlockSpec((1, d), lambda b, i: (0, 0)),
            pl.BlockSpec((d, 4 * d), lambda b, i: (0, 0), **const),
            pl.BlockSpec((CONV_K, d), lambda b, i: (0, 0)),
            pl.BlockSpec((d, d), lambda b, i: (0, 0), **const),
        ],
        out_specs=pl.BlockSpec((1, tm, d), lambda b, i: (b, i, 0)),
        out_shape=jax.ShapeDtypeStruct((bsz, s, d), F32),
        scratch_shapes=[pltpu.VMEM((SUBLANES, d), F32)],
        compiler_params=pltpu.CompilerParams(
            dimension_semantics=("arbitrary", "arbitrary"),
            vmem_limit_bytes=56 * 1024 * 1024),
        name="conv_layer",
    )(x, mod, g, w_in, conv_w, w_out)


def _proj_kernel(x_ref, modkv_ref, modb_ref, gkv_ref, gb_ref, wkv_ref, wqz_ref,
                 q_ref, kv_ref, z_ref):
    d = x_ref.shape[2]
    x = x_ref[0]
    xn = x * lax.rsqrt(jnp.mean(x * x, axis=-1, keepdims=True) + EPS)
    modkv = modkv_ref[0]
    modb = modb_ref[0]
    hkv = (xn * (gkv_ref[...] * (1.0 + modkv[1:2])) + modkv[0:1]).astype(BF16)
    hq = (xn * (gb_ref[...] * (1.0 + modb[1:2])) + modb[0:1]).astype(BF16)
    kv_ref[0] = _dot(hkv, wkv_ref[...]).astype(BF16)
    q_ref[0] = (_dot(hq, wqz_ref[:, :d]) * (LOG2E / math.sqrt(HEAD_DIM))).astype(BF16)
    z_ref[0] = _dot(hq, wqz_ref[:, d:])


def _projections(x, modkv, modb, gkv, gb, w_kv, w_qz, *, tm=1024):
    bsz, s, d = x.shape
    const = dict(pipeline_mode=pl.Buffered(1))
    row = lambda b, i: (b, i, 0)
    return pl.pallas_call(
        _proj_kernel,
        grid=(bsz, s // tm),
        in_specs=[
            pl.BlockSpec((1, tm, d), row),
            pl.BlockSpec((1, 2, d), lambda b, i: (b, 0, 0)),
            pl.BlockSpec((1, 3, d), lambda b, i: (b, 0, 0)),
            pl.BlockSpec((1, d), lambda b, i: (0, 0)),
            pl.BlockSpec((1, d), lambda b, i: (0, 0)),
            pl.BlockSpec((d, 2 * d), lambda b, i: (0, 0), **const),
            pl.BlockSpec((d, 2 * d), lambda b, i: (0, 0), **const),
        ],
        out_specs=[
            pl.BlockSpec((1, tm, d), row),
            pl.BlockSpec((1, tm, 2 * d), row),
            pl.BlockSpec((1, tm, d), row),
        ],
        out_shape=[
            jax.ShapeDtypeStruct((bsz, s, d), BF16),
            jax.ShapeDtypeStruct((bsz, s, 2 * d), BF16),
            jax.ShapeDtypeStruct((bsz, s, d), F32),
        ],
        compiler_params=pltpu.CompilerParams(
            dimension_semantics=("arbitrary", "arbitrary"),
            vmem_limit_bytes=56 * 1024 * 1024),
        name="projections",
    )(x, modkv, modb, gkv, gb, w_kv, w_qz)


def _suffix_weights():
    j = lax.broadcasted_iota(jnp.int32, (K_BLOCK, K_BLOCK), 0)
    s = lax.broadcasted_iota(jnp.int32, (K_BLOCK, K_BLOCK), 1)
    return jnp.where((j > s) | (s == K_BLOCK - 1), -1.0, 0.0).astype(BF16)


def _attn_kernel(q_ref, k_ref, v_ref, z_ref, w_ref, o_ref,
                 qm_ref, acc_ref, carry_ref, z2a_ref, z2b_ref, aa_ref, ab_ref):
    n_q = q_ref.shape[1] // Q_TILE
    n_groups = q_ref.shape[2] // LANES
    heads = tuple(range(2 * n_groups))
    lane = lax.broadcasted_iota(jnp.int32, (1, LANES), 1)
    head0 = lane < HEAD_DIM
    last_key = lax.broadcasted_iota(jnp.int32, (Q_TILE, K_BLOCK), 1) == K_BLOCK - 1
    quadrant = (Q_TILE // 2, K_BLOCK // 2)
    causal_q = (lax.broadcasted_iota(jnp.int32, quadrant, 1)
                < lax.broadcasted_iota(jnp.int32, quadrant, 0))
    w = w_ref[...]

    def block_start(kb):
        return pl.multiple_of(kb * K_BLOCK, K_BLOCK)

    def logits(h, kb, z2_buf):
        g = h // 2
        k_blk = k_ref[0, pl.ds(block_start(kb), K_BLOCK), g * LANES:(g + 1) * LANES]
        z2_buf[h] = lax.dot_general(qm_ref[h], k_blk, (((1,), (1,)), ((), ())),
                                    preferred_element_type=F32)

    def accumulate(h, a_buf, kb):
        g = h // 2
        v_blk = v_ref[0, pl.ds(block_start(kb), K_BLOCK), g * LANES:(g + 1) * LANES]
        acc_ref[h] += _dot(a_buf[h], v_blk)

    half = Q_TILE // 2

    def softplus2(z2):
        return jnp.maximum(z2, 0.0) + jnp.log(1.0 + jnp.exp2(-jnp.abs(z2))) * LOG2E

    def suffix_sums(h, z2_buf, masked):
        if not masked:
            z2 = z2_buf[h]
            sp = softplus2(z2)
            z2_buf[h] = z2 - sp
            return _dot(sp.astype(BF16), w)
        z_tl = z2_buf[h, :half, :half]
        sp_tl = softplus2(z_tl)
        z2_buf[h, :half, :half] = z_tl - sp_tl
        z_bot = z2_buf[h, half:, :]
        sp_bot = softplus2(z_bot)
        z2_buf[h, half:, :] = z_bot - sp_bot
        sp_top = jnp.concatenate([jnp.where(causal_q, sp_tl, 0.0), jnp.zeros_like(sp_tl)], axis=1)
        sp_bot = jnp.concatenate([sp_bot[:, :half], jnp.where(causal_q, sp_bot[:, half:], 0.0)],
                                 axis=1)
        return _dot(jnp.concatenate([sp_top, sp_bot], axis=0).astype(BF16), w)

    def weights(h, r, z2_buf, a_buf, masked):
        carry = carry_ref[h]
        nxt = carry + r[:, K_BLOCK - 1:]
        carry_ref[h] = nxt
        if not masked:
            suffix = jnp.where(last_key, 0.0, r)
            ah = jnp.exp2(z2_buf[h] + (suffix + jnp.concatenate([carry] * (K_BLOCK // LANES), axis=1)))
            a_buf[h] = ah.astype(BF16)
            return nxt
        a_tl = jnp.exp2(z2_buf[h, :half, :half] + (r[:half, :half] + carry[:half]))
        a_bl = jnp.exp2(z2_buf[h, half:, :half] + (r[half:, :half] + carry[half:]))
        a_br = jnp.exp2(z2_buf[h, half:, half:] + (r[half:, half:] + carry[half:]))
        a_top = jnp.concatenate([jnp.where(causal_q, a_tl, 0.0), jnp.zeros_like(a_tl)], axis=1)
        a_bot = jnp.concatenate([a_bl, jnp.where(causal_q, a_br, 0.0)], axis=1)
        a_buf[h] = jnp.concatenate([a_top, a_bot], axis=0).astype(BF16)
        return nxt

    def block_step(kb, kb_prev, variant, masked):
        z2_cur, z2_next = (z2a_ref, z2b_ref) if variant == 0 else (z2b_ref, z2a_ref)
        a_cur, a_prev = (aa_ref, ab_ref) if variant == 0 else (ab_ref, aa_ref)
        if masked:
            for h in heads:
                logits(h, kb, z2_cur)
        r = {}
        worst = None
        for step in range(len(heads) + WEIGHTS_LAG):
            if step < len(heads):
                h = heads[step]
                logits(h, jnp.maximum(kb - 1, 0), z2_next)
                accumulate(h, a_prev, kb_prev)
                r[h] = suffix_sums(h, z2_cur, masked)
            if step >= WEIGHTS_LAG:
                h = heads[step - WEIGHTS_LAG]
                nxt = weights(h, r.pop(h), z2_cur, a_cur, masked)
                worst = nxt if worst is None else jnp.maximum(worst, nxt)
        return jnp.max(worst)

    def alternate(variant, fn):
        return lax.cond(variant == 0, lambda: fn(0), lambda: fn(1))

    def finalize(qstart):
        for g in range(n_groups):
            lanes = slice(g * LANES, (g + 1) * LANES)
            zg = z_ref[0, pl.ds(qstart, Q_TILE), lanes]
            acc = jnp.where(head0, acc_ref[2 * g], acc_ref[2 * g + 1])
            o_ref[0, pl.ds(qstart, Q_TILE), lanes] = (acc * _silu(zg)).astype(o_ref.dtype)
        acc_ref[...] = jnp.zeros_like(acc_ref)

    def q_tile(qi, state):
        variant, kb_prev = state
        qstart = pl.multiple_of(qi * Q_TILE, Q_TILE)
        for g in range(n_groups):
            q = q_ref[0, pl.ds(qstart, Q_TILE), g * LANES:(g + 1) * LANES]
            zero = jnp.zeros_like(q)
            qm_ref[2 * g] = jnp.where(head0, q, zero)
            qm_ref[2 * g + 1] = jnp.where(head0, zero, q)
        carry_ref[...] = jnp.zeros_like(carry_ref)

        worst = alternate(variant, lambda v: block_step(qi, kb_prev, v, True))

        @pl.when(qi > 0)
        def _():
            finalize(pl.multiple_of((qi - 1) * Q_TILE, Q_TILE))

        def more_blocks(state):
            j, worst, _ = state
            return jnp.logical_and(j < qi, worst > NEGLIGIBLE_LOG2)

        def below_diagonal(state):
            j, _, variant = state
            kb = qi - 1 - j
            worst = alternate(variant, lambda v: block_step(kb, kb + 1, v, False))
            return j + 1, worst, 1 - variant

        n_below, _, variant = lax.while_loop(more_blocks, below_diagonal,
                                             (jnp.int32(0), worst, 1 - variant))
        return variant, qi - n_below

    aa_ref[...] = jnp.zeros_like(aa_ref)
    ab_ref[...] = jnp.zeros_like(ab_ref)
    acc_ref[...] = jnp.zeros_like(acc_ref)
    variant, kb_prev = lax.fori_loop(0, n_q, q_tile, (jnp.int32(0), jnp.int32(0)))

    def drain(v):
        a_prev = ab_ref if v == 0 else aa_ref
        for h in heads:
            accumulate(h, a_prev, kb_prev)

    alternate(variant, drain)
    finalize((n_q - 1) * Q_TILE)


def _attention(q, kv, z, *, groups_per_step=4):
    bsz, s, d = q.shape
    width = groups_per_step * LANES
    n_steps = d // width
    n_heads = 2 * groups_per_step
    grp = lambda b, g: (b, 0, g)
    return pl.pallas_call(
        _attn_kernel,
        grid=(bsz, n_steps),
        in_specs=[
            pl.BlockSpec((1, s, width), grp),
            pl.BlockSpec((1, s, width), grp),
            pl.BlockSpec((1, s, width), lambda b, g: (b, 0, n_steps + g)),
            pl.BlockSpec((1, s, width), grp),
            pl.BlockSpec((K_BLOCK, K_BLOCK), lambda b, g: (0, 0)),
        ],
        out_specs=pl.BlockSpec((1, s, width), grp),
        out_shape=jax.ShapeDtypeStruct((bsz, s, d), BF16),
        scratch_shapes=[
            pltpu.VMEM((n_heads, Q_TILE, LANES), BF16),
            pltpu.VMEM((n_heads, Q_TILE, LANES), F32),
            pltpu.VMEM((n_heads, Q_TILE, LANES), F32),
            pltpu.VMEM((n_heads, Q_TILE, K_BLOCK), F32),
            pltpu.VMEM((n_heads, Q_TILE, K_BLOCK), F32),
            pltpu.VMEM((n_heads, Q_TILE, K_BLOCK), BF16),
            pltpu.VMEM((n_heads, Q_TILE, K_BLOCK), BF16),
        ],
        compiler_params=pltpu.CompilerParams(
            dimension_semantics=("arbitrary", "arbitrary"),
            vmem_limit_bytes=40 * 1024 * 1024),
        name="stickbreak_attention",
    )(q, kv, kv, z, _suffix_weights())


def _out_kernel(o_ref, x_ref, mod_ref, wout_ref, g_ref, y_ref):
    gate = mod_ref[0][2:3]
    x = x_ref[0] + gate * _dot(o_ref[0], wout_ref[...])
    y_ref[0] = (x * lax.rsqrt(jnp.mean(x * x, axis=-1, keepdims=True) + EPS)) * g_ref[...]


def _out_layer(o, x, modb, w_out, g, *, tm=1024):
    bsz, s, d = x.shape
    row = lambda b, i: (b, i, 0)
    return pl.pallas_call(
        _out_kernel,
        grid=(bsz, s // tm),
        in_specs=[
            pl.BlockSpec((1, tm, d), row),
            pl.BlockSpec((1, tm, d), row),
            pl.BlockSpec((1, 3, d), lambda b, i: (b, 0, 0)),
            pl.BlockSpec((d, d), lambda b, i: (0, 0)),
            pl.BlockSpec((1, d), lambda b, i: (0, 0)),
        ],
        out_specs=pl.BlockSpec((1, tm, d), row),
        out_shape=jax.ShapeDtypeStruct((bsz, s, d), F32),
        compiler_params=pltpu.CompilerParams(
            dimension_semantics=("arbitrary", "arbitrary"),
            vmem_limit_bytes=32 * 1024 * 1024),
        name="out_layer",
    )(o, x, modb, w_out, g)


def kernel(x, c, a_mod_w, a_mod_b, a_norm_g, a_w_in, a_conv_w, a_w_out,
           kv_mod_w, kv_mod_b, kv_norm_g, w_kv,
           b_mod_w, b_mod_b, b_norm_g, b_w_qz, b_w_out, final_norm_g):
    bsz, _, d = x.shape
    assert d == D_MODEL and a_mod_w.shape[0] == 1 and b_mod_w.shape[0] == 1

    mod = _modulation(c, [a_mod_w.reshape(d, 3 * d), kv_mod_w, b_mod_w.reshape(d, 3 * d)],
                      [a_mod_b, kv_mod_b, b_mod_b])
    mod_a = mod[:, :3 * d].reshape(bsz, 3, d)
    mod_kv = mod[:, 3 * d:5 * d].reshape(bsz, 2, d)
    mod_b = mod[:, 5 * d:].reshape(bsz, 3, d)

    x1 = _conv_layer(x, mod_a, a_norm_g, a_w_in[0].astype(BF16), a_conv_w[0],
                     a_w_out[0].astype(BF16))
    q, kv, z = _projections(x1, mod_kv, mod_b, kv_norm_g.reshape(1, d), b_norm_g,
                            w_kv.astype(BF16), b_w_qz[0].astype(BF16))
    o = _attention(q, kv, z)
    return _out_layer(o, x1, mod_b, b_w_out[0].astype(BF16), final_norm_g.reshape(1, d))
```

```python
import functools
import math

import jax
import jax.numpy as jnp
from jax import lax
from jax.experimental import pallas as pl
from jax.experimental.pallas import tpu as pltpu

D_MODEL = 1024
N_HEADS = 16
HEAD_DIM = D_MODEL // N_HEADS
CONV_K = 3
EPS = 1e-6
LOG2E = 1.4426950408889634

LANES = 128
SUBLANES = 8
MXU_DIM = 256

K_BLOCK = MXU_DIM
Q_TILE = K_BLOCK
NEGLIGIBLE_LOG2 = -160.0
WEIGHTS_LAG = 2

BF16 = jnp.bfloat16
F32 = jnp.float32


def _dot(a, b):
    return jnp.dot(a, b, preferred_element_type=F32)


def _silu(x):
    return x * jax.nn.sigmoid(x)


def _mod_kernel(c_ref, *refs, starts):
    n = len(starts) - 1
    w_refs, b_refs, o_ref = refs[:n], refs[n:2 * n], refs[2 * n]
    j = pl.program_id(0)
    s = _silu(c_ref[...]).astype(BF16)
    for i in range(n):
        @pl.when(jnp.logical_and(j >= starts[i], j < starts[i + 1]))
        def _(i=i):
            o_ref[...] = _dot(s, w_refs[i][...].astype(BF16)) + b_refs[i][...]


def _modulation(c, ws, bs, *, tn=1024):
    bsz, d = c.shape
    tiles = [w.shape[1] // tn for w in ws]
    starts = [sum(tiles[:i]) for i in range(len(ws) + 1)]

    def tile_of(i):
        return lambda j: (0, jnp.clip(j - starts[i], 0, tiles[i] - 1))

    return pl.pallas_call(
        functools.partial(_mod_kernel, starts=tuple(starts)),
        grid=(starts[-1],),
        in_specs=([pl.BlockSpec((bsz, d), lambda j: (0, 0))]
                  + [pl.BlockSpec((d, tn), tile_of(i)) for i in range(len(ws))]
                  + [pl.BlockSpec((1, tn), tile_of(i)) for i in range(len(ws))]),
        out_specs=pl.BlockSpec((bsz, tn), lambda j: (0, j)),
        out_shape=jax.ShapeDtypeStruct((bsz, starts[-1] * tn), F32),
        compiler_params=pltpu.CompilerParams(
            dimension_semantics=("arbitrary",),
            vmem_limit_bytes=40 * 1024 * 1024),
        name="modulation",
    )(c, *ws, *[b.reshape(1, -1) for b in bs])


def _conv_layer_kernel(x_ref, mod_ref, g_ref, win_ref, cw_ref, wout_ref, o_ref, carry_ref):
    tm = x_ref.shape[1]
    d = x_ref.shape[2]

    @pl.when(pl.program_id(1) == 0)
    def _():
        carry_ref[...] = jnp.zeros_like(carry_ref)

    x = x_ref[0]
    mod = mod_ref[0]
    shift, scale, gate = mod[0:1], mod[1:2], mod[2:3]
    rs = lax.rsqrt(jnp.mean(x * x, axis=-1, keepdims=True) + EPS)
    h = ((x * rs) * (g_ref[...] * (1.0 + scale)) + shift).astype(BF16)

    b_gate = _dot(h, win_ref[:, 0 * d:1 * d].astype(BF16))
    c_gate = _dot(h, win_ref[:, 1 * d:2 * d].astype(BF16))
    u = _dot(h, win_ref[:, 2 * d:3 * d].astype(BF16))
    z = _dot(h, win_ref[:, 3 * d:4 * d].astype(BF16))

    cu = c_gate * u
    ext = jnp.concatenate([carry_ref[...], cu], axis=0)
    prev1 = pltpu.roll(ext, 1, 0)[SUBLANES:]
    prev2 = pltpu.roll(ext, 2, 0)[SUBLANES:]
    carry_ref[...] = cu[tm - SUBLANES:]
    cw = cw_ref[...]
    conv = cw[0:1] * prev2 + cw[1:2] * prev1 + cw[2:3] * cu

    y = (b_gate * conv) * _silu(z)
    o_ref[0] = x + gate * _dot(y.astype(BF16), wout_ref[...].astype(BF16))


def _conv_layer(x, mod, g, w_in, conv_w, w_out, *, tm=1024):
    bsz, s, d = x.shape
    const = dict(pipeline_mode=pl.Buffered(1))
    return pl.pallas_call(
        _conv_layer_kernel,
        grid=(bsz, s // tm),
        in_specs=[
            pl.BlockSpec((1, tm, d), lambda b, i: (b, i, 0)),
            pl.BlockSpec((1, 3, d), lambda b, i: (b, 0, 0)),
            pl.BlockSpec((1, d), lambda b, i: (0, 0)),
            pl.BlockSpec((d, 4 * d), lambda b, i: (0, 0), **const),
            pl.BlockSpec((CONV_K, d), lambda b, i: (0, 0)),
            pl.BlockSpec((d, d), lambda b, i: (0, 0), **const),
        ],
        out_specs=pl.BlockSpec((1, tm, d), lambda b, i: (b, i, 0)),
        out_shape=jax.ShapeDtypeStruct((bsz, s, d), F32),
        scratch_shapes=[pltpu.VMEM((SUBLANES, d), F32)],
        compiler_params=pltpu.CompilerParams(
            dimension_semantics=("arbitrary", "arbitrary"),
            vmem_limit_bytes=56 * 1024 * 1024),
        name="conv_layer",
    )(x, mod, g, w_in, conv_w, w_out)


def _proj_kernel(x_ref, modkv_ref, modb_ref, gkv_ref, gb_ref, wkv_ref, wqz_ref,
                 q_ref, kv_ref, z_ref):
    d = x_ref.shape[2]
    x = x_ref[0]
    xn = x * lax.rsqrt(jnp.mean(x * x, axis=-1, keepdims=True) + EPS)
    modkv = modkv_ref[0]
    modb = modb_ref[0]
    hkv = (xn * (gkv_ref[...] * (1.0 + modkv[1:2])) + modkv[0:1]).astype(BF16)
    hq = (xn * (gb_ref[...] * (1.0 + modb[1:2])) + modb[0:1]).astype(BF16)
    kv_ref[0] = _dot(hkv, wkv_ref[...].astype(BF16)).astype(BF16)
    q_ref[0] = (_dot(hq, wqz_ref[:, :d].astype(BF16)) * (LOG2E / math.sqrt(HEAD_DIM))).astype(BF16)
    z_ref[0] = _dot(hq, wqz_ref[:, d:].astype(BF16))


def _projections(x, modkv, modb, gkv, gb, w_kv, w_qz, *, tm=1024):
    bsz, s, d = x.shape
    const = dict(pipeline_mode=pl.Buffered(1))
    row = lambda b, i: (b, i, 0)
    return pl.pallas_call(
        _proj_kernel,
        grid=(bsz, s // tm),
        in_specs=[
            pl.BlockSpec((1, tm, d), row),
            pl.BlockSpec((1, 2, d), lambda b, i: (b, 0, 0)),
            pl.BlockSpec((1, 3, d), lambda b, i: (b, 0, 0)),
            pl.BlockSpec((1, d), lambda b, i: (0, 0)),
            pl.BlockSpec((1, d), lambda b, i: (0, 0)),
            pl.BlockSpec((d, 2 * d), lambda b, i: (0, 0), **const),
            pl.BlockSpec((d, 2 * d), lambda b, i: (0, 0), **const),
        ],
        out_specs=[
            pl.BlockSpec((1, tm, d), row),
            pl.BlockSpec((1, tm, 2 * d), row),
            pl.BlockSpec((1, tm, d), row),
        ],
        out_shape=[
            jax.ShapeDtypeStruct((bsz, s, d), BF16),
            jax.ShapeDtypeStruct((bsz, s, 2 * d), BF16),
            jax.ShapeDtypeStruct((bsz, s, d), F32),
        ],
        compiler_params=pltpu.CompilerParams(
            dimension_semantics=("arbitrary", "arbitrary"),
            vmem_limit_bytes=56 * 1024 * 1024),
        name="projections",
    )(x, modkv, modb, gkv, gb, w_kv, w_qz)


def _suffix_weights():
    j = lax.broadcasted_iota(jnp.int32, (K_BLOCK, K_BLOCK), 0)
    s = lax.broadcasted_iota(jnp.int32, (K_BLOCK, K_BLOCK), 1)
    return jnp.where((j > s) | (s == K_BLOCK - 1), -1.0, 0.0).astype(BF16)


def _attn_kernel(q_ref, k_ref, v_ref, z_ref, w_ref, o_ref,
                 qm_ref, acc_ref, carry_ref, z2a_ref, z2b_ref, aa_ref, ab_ref):
    n_q = q_ref.shape[1] // Q_TILE
    n_groups = q_ref.shape[2] // LANES
    heads = tuple(range(2 * n_groups))
    lane = lax.broadcasted_iota(jnp.int32, (1, LANES), 1)
    head0 = lane < HEAD_DIM
    last_key = lax.broadcasted_iota(jnp.int32, (Q_TILE, K_BLOCK), 1) == K_BLOCK - 1
    quadrant = (Q_TILE // 2, K_BLOCK // 2)
    causal_q = (lax.broadcasted_iota(jnp.int32, quadrant, 1)
                < lax.broadcasted_iota(jnp.int32, quadrant, 0))
    w = w_ref[...]

    def block_start(kb):
        return pl.multiple_of(kb * K_BLOCK, K_BLOCK)

    def logits(h, kb, z2_buf):
        g = h // 2
        k_blk = k_ref[0, pl.ds(block_start(kb), K_BLOCK), g * LANES:(g + 1) * LANES]
        z2_buf[h] = lax.dot_general(qm_ref[h], k_blk, (((1,), (1,)), ((), ())),
                                    preferred_element_type=F32)

    def accumulate(h, a_buf, kb):
        g = h // 2
        v_blk = v_ref[0, pl.ds(block_start(kb), K_BLOCK), g * LANES:(g + 1) * LANES]
        acc_ref[h] += _dot(a_buf[h], v_blk)

    half = Q_TILE // 2

    def softplus2(z2):
        return jnp.maximum(z2, 0.0) + jnp.log(1.0 + jnp.exp2(-jnp.abs(z2))) * LOG2E

    def suffix_sums(h, z2_buf, masked):
        if not masked:
            z2 = z2_buf[h]
            sp = softplus2(z2)
            z2_buf[h] = z2 - sp
            return _dot(sp.astype(BF16), w)
        z_tl = z2_buf[h, :half, :half]
        sp_tl = softplus2(z_tl)
        z2_buf[h, :half, :half] = z_tl - sp_tl
        z_bot = z2_buf[h, half:, :]
        sp_bot = softplus2(z_bot)
        z2_buf[h, half:, :] = z_bot - sp_bot
        sp_top = jnp.concatenate([jnp.where(causal_q, sp_tl, 0.0), jnp.zeros_like(sp_tl)], axis=1)
        sp_bot = jnp.concatenate([sp_bot[:, :half], jnp.where(causal_q, sp_bot[:, half:], 0.0)],
                                 axis=1)
        return _dot(jnp.concatenate([sp_top, sp_bot], axis=0).astype(BF16), w)

    def weights(h, r, z2_buf, a_buf, masked):
        carry = carry_ref[h]
        nxt = carry + r[:, K_BLOCK - 1:]
        carry_ref[h] = nxt
        if not masked:
            suffix = jnp.where(last_key, 0.0, r)
            ah = jnp.exp2(z2_buf[h] + (suffix + jnp.concatenate([carry] * (K_BLOCK // LANES), axis=1)))
            a_buf[h] = ah.astype(BF16)
            return nxt
        a_tl = jnp.exp2(z2_buf[h, :half, :half] + (r[:half, :half] + carry[:half]))
        a_bl = jnp.exp2(z2_buf[h, half:, :half] + (r[half:, :half] + carry[half:]))
        a_br = jnp.exp2(z2_buf[h, half:, half:] + (r[half:, half:] + carry[half:]))
        a_top = jnp.concatenate([jnp.where(causal_q, a_tl, 0.0), jnp.zeros_like(a_tl)], axis=1)
        a_bot = jnp.concatenate([a_bl, jnp.where(causal_q, a_br, 0.0)], axis=1)
        a_buf[h] = jnp.concatenate([a_top, a_bot], axis=0).astype(BF16)
        return nxt

    def block_step(kb, kb_prev, variant, masked):
        z2_cur, z2_next = (z2a_ref, z2b_ref) if variant == 0 else (z2b_ref, z2a_ref)
        a_cur, a_prev = (aa_ref, ab_ref) if variant == 0 else (ab_ref, aa_ref)
        if masked:
            for h in heads:
                logits(h, kb, z2_cur)
        r = {}
        worst = None
        for step in range(len(heads) + WEIGHTS_LAG):
            if step < len(heads):
                h = heads[step]
                logits(h, jnp.maximum(kb - 1, 0), z2_next)
                accumulate(h, a_prev, kb_prev)
                r[h] = suffix_sums(h, z2_cur, masked)
            if step >= WEIGHTS_LAG:
                h = heads[step - WEIGHTS_LAG]
                nxt = weights(h, r.pop(h), z2_cur, a_cur, masked)
                worst = nxt if worst is None else jnp.maximum(worst, nxt)
        return jnp.max(worst)

    def alternate(variant, fn):
        return lax.cond(variant == 0, lambda: fn(0), lambda: fn(1))

    def finalize(qstart):
        for g in range(n_groups):
            lanes = slice(g * LANES, (g + 1) * LANES)
            zg = z_ref[0, pl.ds(qstart, Q_TILE), lanes]
            acc = jnp.where(head0, acc_ref[2 * g], acc_ref[2 * g + 1])
            o_ref[0, pl.ds(qstart, Q_TILE), lanes] = (acc * _silu(zg)).astype(o_ref.dtype)
        acc_ref[...] = jnp.zeros_like(acc_ref)

    def q_tile(qi, state):
        variant, kb_prev = state
        qstart = pl.multiple_of(qi * Q_TILE, Q_TILE)
        for g in range(n_groups):
            q = q_ref[0, pl.ds(qstart, Q_TILE), g * LANES:(g + 1) * LANES]
            zero = jnp.zeros_like(q)
            qm_ref[2 * g] = jnp.where(head0, q, zero)
            qm_ref[2 * g + 1] = jnp.where(head0, zero, q)
        carry_ref[...] = jnp.zeros_like(carry_ref)

        worst = alternate(variant, lambda v: block_step(qi, kb_prev, v, True))

        @pl.when(qi > 0)
        def _():
            finalize(pl.multiple_of((qi - 1) * Q_TILE, Q_TILE))

        def more_blocks(state):
            j, worst, _ = state
            return jnp.logical_and(j < qi, worst > NEGLIGIBLE_LOG2)

        def below_diagonal(state):
            j, _, variant = state
            kb = qi - 1 - j
            worst = alternate(variant, lambda v: block_step(kb, kb + 1, v, False))
            return j + 1, worst, 1 - variant

        n_below, _, variant = lax.while_loop(more_blocks, below_diagonal,
                                             (jnp.int32(0), worst, 1 - variant))
        return variant, qi - n_below

    aa_ref[...] = jnp.zeros_like(aa_ref)
    ab_ref[...] = jnp.zeros_like(ab_ref)
    acc_ref[...] = jnp.zeros_like(acc_ref)
    variant, kb_prev = lax.fori_loop(0, n_q, q_tile, (jnp.int32(0), jnp.int32(0)))

    def drain(v):
        a_prev = ab_ref if v == 0 else aa_ref
        for h in heads:
            accumulate(h, a_prev, kb_prev)

    alternate(variant, drain)
    finalize((n_q - 1) * Q_TILE)


def _attention(q, kv, z, *, groups_per_step=4):
    bsz, s, d = q.shape
    width = groups_per_step * LANES
    n_steps = d // width
    n_heads = 2 * groups_per_step
    grp = lambda b, g: (b, 0, g)
    return pl.pallas_call(
        _attn_kernel,
        grid=(bsz, n_steps),
        in_specs=[
            pl.BlockSpec((1, s, width), grp),
            pl.BlockSpec((1, s, width), grp),
            pl.BlockSpec((1, s, width), lambda b, g: (b, 0, n_steps + g)),
            pl.BlockSpec((1, s, width), grp),
            pl.BlockSpec((K_BLOCK, K_BLOCK), lambda b, g: (0, 0)),
        ],
        out_specs=pl.BlockSpec((1, s, width), grp),
        out_shape=jax.ShapeDtypeStruct((bsz, s, d), BF16),
        scratch_shapes=[
            pltpu.VMEM((n_heads, Q_TILE, LANES), BF16),
            pltpu.VMEM((n_heads, Q_TILE, LANES), F32),
            pltpu.VMEM((n_heads, Q_TILE, LANES), F32),
            pltpu.VMEM((n_heads, Q_TILE, K_BLOCK), F32),
            pltpu.VMEM((n_heads, Q_TILE, K_BLOCK), F32),
            pltpu.VMEM((n_heads, Q_TILE, K_BLOCK), BF16),
            pltpu.VMEM((n_heads, Q_TILE, K_BLOCK), BF16),
        ],
        compiler_params=pltpu.CompilerParams(
            dimension_semantics=("arbitrary", "arbitrary"),
            vmem_limit_bytes=40 * 1024 * 1024),
        name="stickbreak_attention",
    )(q, kv, kv, z, _suffix_weights())


def _out_kernel(o_ref, x_ref, mod_ref, wout_ref, g_ref, y_ref):
    gate = mod_ref[0][2:3]
    x = x_ref[0] + gate * _dot(o_ref[0], wout_ref[...].astype(BF16))
    y_ref[0] = (x * lax.rsqrt(jnp.mean(x * x, axis=-1, keepdims=True) + EPS)) * g_ref[...]


def _out_layer(o, x, modb, w_out, g, *, tm=1024):
    bsz, s, d = x.shape
    row = lambda b, i: (b, i, 0)
    return pl.pallas_call(
        _out_kernel,
        grid=(bsz, s // tm),
        in_specs=[
            pl.BlockSpec((1, tm, d), row),
            pl.BlockSpec((1, tm, d), row),
            pl.BlockSpec((1, 3, d), lambda b, i: (b, 0, 0)),
            pl.BlockSpec((d, d), lambda b, i: (0, 0)),
            pl.BlockSpec((1, d), lambda b, i: (0, 0)),
        ],
        out_specs=pl.BlockSpec((1, tm, d), row),
        out_shape=jax.ShapeDtypeStruct((bsz, s, d), F32),
        compiler_params=pltpu.CompilerParams(
            dimension_semantics=("arbitrary", "arbitrary"),
            vmem_limit_bytes=32 * 1024 * 1024),
        name="out_layer",
    )(o, x, modb, w_out, g)


def kernel(x, c, a_mod_w, a_mod_b, a_norm_g, a_w_in, a_conv_w, a_w_out,
           kv_mod_w, kv_mod_b, kv_norm_g, w_kv,
           b_mod_w, b_mod_b, b_norm_g, b_w_qz, b_w_out, final_norm_g):
    bsz, _, d = x.shape
    assert d == D_MODEL and a_mod_w.shape[0] == 1 and b_mod_w.shape[0] == 1

    mod = _modulation(c, [a_mod_w.reshape(d, 3 * d), kv_mod_w, b_mod_w.reshape(d, 3 * d)],
                      [a_mod_b, kv_mod_b, b_mod_b])
    mod_a = mod[:, :3 * d].reshape(bsz, 3, d)
    mod_kv = mod[:, 3 * d:5 * d].reshape(bsz, 2, d)
    mod_b = mod[:, 5 * d:].reshape(bsz, 3, d)

    x1 = _conv_layer(x, mod_a, a_norm_g, a_w_in[0], a_conv_w[0], a_w_out[0])
    q, kv, z = _projections(x1, mod_kv, mod_b, kv_norm_g.reshape(1, d), b_norm_g,
                            w_kv, b_w_qz[0])
    o = _attention(q, kv, z)
    return _out_layer(o, x1, mod_b, b_w_out[0], final_norm_g.reshape(1, d))
```

```python
import functools
import math

import jax
import jax.numpy as jnp
from jax import lax
from jax.experimental import pallas as pl
from jax.experimental.pallas import tpu as pltpu

D_MODEL = 1024
N_HEADS = 16
HEAD_DIM = D_MODEL // N_HEADS
CONV_K = 3
EPS = 1e-6
LOG2E = 1.4426950408889634

LANES = 128
SUBLANES = 8
MXU_DIM = 256

K_BLOCK = MXU_DIM
Q_TILE = K_BLOCK
NEGLIGIBLE_LOG2 = -160.0
WEIGHTS_LAG = 2

BF16 = jnp.bfloat16
F32 = jnp.float32


def _dot(a, b):
    return jnp.dot(a, b, preferred_element_type=F32)


def _silu(x):
    return x * jax.nn.sigmoid(x)


def _mod_kernel(c_ref, *refs, starts):
    n = len(starts) - 1
    w_refs, b_refs, o_ref = refs[:n], refs[n:2 * n], refs[2 * n]
    j = pl.program_id(0)
    s = _silu(c_ref[...]).astype(BF16)
    for i in range(n):
        @pl.when(jnp.logical_and(j >= starts[i], j < starts[i + 1]))
        def _(i=i):
            o_ref[...] = _dot(s, w_refs[i][...].astype(BF16)) + b_refs[i][...]


def _modulation(c, ws, bs, *, tn=1024):
    bsz, d = c.shape
    tiles = [w.shape[1] // tn for w in ws]
    starts = [sum(tiles[:i]) for i in range(len(ws) + 1)]

    def tile_of(i):
        return lambda j: (0, jnp.clip(j - starts[i], 0, tiles[i] - 1))

    return pl.pallas_call(
        functools.partial(_mod_kernel, starts=tuple(starts)),
        grid=(starts[-1],),
        in_specs=([pl.BlockSpec((bsz, d), lambda j: (0, 0))]
                  + [pl.BlockSpec((d, tn), tile_of(i)) for i in range(len(ws))]
                  + [pl.BlockSpec((1, tn), tile_of(i)) for i in range(len(ws))]),
        out_specs=pl.BlockSpec((bsz, tn), lambda j: (0, j)),
        out_shape=jax.ShapeDtypeStruct((bsz, starts[-1] * tn), F32),
        compiler_params=pltpu.CompilerParams(
            dimension_semantics=("arbitrary",),
            vmem_limit_bytes=40 * 1024 * 1024),
        name="modulation",
    )(c, *ws, *[b.reshape(1, -1) for b in bs])


def _conv_layer_kernel(x_ref, mod_ref, g_ref, win_ref, cw_ref, wout_ref, o_ref, carry_ref):
    tm = x_ref.shape[1]
    d = x_ref.shape[2]

    @pl.when(pl.program_id(1) == 0)
    def _():
        carry_ref[...] = jnp.zeros_like(carry_ref)

    x = x_ref[0]
    mod = mod_ref[0]
    shift, scale, gate = mod[0:1], mod[1:2], mod[2:3]
    rs = lax.rsqrt(jnp.mean(x * x, axis=-1, keepdims=True) + EPS)
    h = ((x * rs) * (g_ref[...] * (1.0 + scale)) + shift).astype(BF16)

    b_gate = _dot(h, win_ref[:, 0 * d:1 * d].astype(BF16))
    c_gate = _dot(h, win_ref[:, 1 * d:2 * d].astype(BF16))
    u = _dot(h, win_ref[:, 2 * d:3 * d].astype(BF16))
    z = _dot(h, win_ref[:, 3 * d:4 * d].astype(BF16))

    cu = c_gate * u
    ext = jnp.concatenate([carry_ref[...], cu], axis=0)
    prev1 = pltpu.roll(ext, 1, 0)[SUBLANES:]
    prev2 = pltpu.roll(ext, 2, 0)[SUBLANES:]
    carry_ref[...] = cu[tm - SUBLANES:]
    cw = cw_ref[...]
    conv = cw[0:1] * prev2 + cw[1:2] * prev1 + cw[2:3] * cu

    y = (b_gate * conv) * _silu(z)
    o_ref[0] = x + gate * _dot(y.astype(BF16), wout_ref[...].astype(BF16))


def _conv_layer(x, mod, g, w_in, conv_w, w_out, *, tm=1024):
    bsz, s, d = x.shape
    const = dict(pipeline_mode=pl.Buffered(1))
    return pl.pallas_call(
        _conv_layer_kernel,
        grid=(bsz, s // tm),
        in_specs=[
            pl.BlockSpec((1, tm, d), lambda b, i: (b, i, 0)),
            pl.BlockSpec((1, 3, d), lambda b, i: (b, 0, 0)),
            pl.BlockSpec((1, d), lambda b, i: (0, 0)),
            pl.BlockSpec((d, 4 * d), lambda b, i: (0, 0), **const),
            pl.BlockSpec((CONV_K, d), lambda b, i: (0, 0)),
            pl.BlockSpec((d, d), lambda b, i: (0, 0), **const),
        ],
        out_specs=pl.BlockSpec((1, tm, d), lambda b, i: (b, i, 0)),
        out_shape=jax.ShapeDtypeStruct((bsz, s, d), F32),
        scratch_shapes=[pltpu.VMEM((SUBLANES, d), F32)],
        compiler_params=pltpu.CompilerParams(
            dimension_semantics=("arbitrary", "arbitrary"),
            vmem_limit_bytes=56 * 1024 * 1024),
        name="conv_layer",
    )(x, mod, g, w_in, conv_w, w_out)


def _proj_kernel(x_ref, modkv_ref, modb_ref, gkv_ref, gb_ref, wkv_ref, wqz_ref,
                 q_ref, kv_ref, z_ref):
    d = x_ref.shape[2]
    x = x_ref[0]
    xn = x * lax.rsqrt(jnp.mean(x * x, axis=-1, keepdims=True) + EPS)
    modkv = modkv_ref[0]
    modb = modb_ref[0]
    hkv = (xn * (gkv_ref[...] * (1.0 + modkv[1:2])) + modkv[0:1]).astype(BF16)
    hq = (xn * (gb_ref[...] * (1.0 + modb[1:2])) + modb[0:1]).astype(BF16)
    kv_ref[0] = _dot(hkv, wkv_ref[...].astype(BF16)).astype(BF16)
    q_ref[0] = (_dot(hq, wqz_ref[:, :d].astype(BF16)) * (LOG2E / math.sqrt(HEAD_DIM))).astype(BF16)
    z_ref[0] = _dot(hq, wqz_ref[:, d:].astype(BF16))


def _projections(x, modkv, modb, gkv, gb, w_kv, w_qz, *, tm=1024):
    bsz, s, d = x.shape
    const = dict(pipeline_mode=pl.Buffered(1))
    row = lambda b, i: (b, i, 0)
    return pl.pallas_call(
        _proj_kernel,
        grid=(bsz, s // tm),
        in_specs=[
            pl.BlockSpec((1, tm, d), row),
            pl.BlockSpec((1, 2, d), lambda b, i: (b, 0, 0)),
            pl.BlockSpec((1, 3, d), lambda b, i: (b, 0, 0)),
            pl.BlockSpec((1, d), lambda b, i: (0, 0)),
            pl.BlockSpec((1, d), lambda b, i: (0, 0)),
            pl.BlockSpec((d, 2 * d), lambda b, i: (0, 0), **const),
            pl.BlockSpec((d, 2 * d), lambda b, i: (0, 0), **const),
        ],
        out_specs=[
            pl.BlockSpec((1, tm, d), row),
            pl.BlockSpec((1, tm, 2 * d), row),
            pl.BlockSpec((1, tm, d), row),
        ],
        out_shape=[
            jax.ShapeDtypeStruct((bsz, s, d), BF16),
            jax.ShapeDtypeStruct((bsz, s, 2 * d), BF16),
            jax.ShapeDtypeStruct((bsz, s, d), F32),
        ],
        compiler_params=pltpu.CompilerParams(
            dimension_semantics=("arbitrary", "arbitrary"),
            vmem_limit_bytes=56 * 1024 * 1024),
        name="projections",
    )(x, modkv, modb, gkv, gb, w_kv, w_qz)


def _suffix_weights():
    j = lax.broadcasted_iota(jnp.int32, (K_BLOCK, K_BLOCK), 0)
    s = lax.broadcasted_iota(jnp.int32, (K_BLOCK, K_BLOCK), 1)
    return jnp.where((j > s) | (s == K_BLOCK - 1), -1.0, 0.0).astype(BF16)


def _attn_kernel(q_ref, k_ref, v_ref, z_ref, w_ref, o_ref,
                 qm_ref, acc_ref, carry_ref, z2a_ref, z2b_ref, aa_ref, ab_ref):
    n_q = q_ref.shape[1] // Q_TILE
    n_groups = q_ref.shape[2] // LANES
    heads = tuple(range(2 * n_groups))
    lane = lax.broadcasted_iota(jnp.int32, (1, LANES), 1)
    head0 = lane < HEAD_DIM
    last_key = lax.broadcasted_iota(jnp.int32, (Q_TILE, K_BLOCK), 1) == K_BLOCK - 1
    quadrant = (Q_TILE // 2, K_BLOCK // 2)
    causal_q = (lax.broadcasted_iota(jnp.int32, quadrant, 1)
                < lax.broadcasted_iota(jnp.int32, quadrant, 0))
    w = w_ref[...]

    def block_start(kb):
        return pl.multiple_of(kb * K_BLOCK, K_BLOCK)

    def logits(h, kb, z2_buf):
        g = h // 2
        k_blk = k_ref[0, pl.ds(block_start(kb), K_BLOCK), g * LANES:(g + 1) * LANES]
        z2_buf[h] = lax.dot_general(qm_ref[h], k_blk, (((1,), (1,)), ((), ())),
                                    preferred_element_type=F32)

    def accumulate(h, a_buf, kb):
        g = h // 2
        v_blk = v_ref[0, pl.ds(block_start(kb), K_BLOCK), g * LANES:(g + 1) * LANES]
        acc_ref[h] += _dot(a_buf[h], v_blk)

    half = Q_TILE // 2

    def softplus2(z2):
        return jnp.maximum(z2, 0.0) + jnp.log(1.0 + jnp.exp2(-jnp.abs(z2))) * LOG2E

    def suffix_sums(h, z2_buf, masked):
        if not masked:
            z2 = z2_buf[h]
            sp = softplus2(z2)
            z2_buf[h] = z2 - sp
            return _dot(sp.astype(BF16), w)
        z_tl = z2_buf[h, :half, :half]
        sp_tl = softplus2(z_tl)
        z2_buf[h, :half, :half] = z_tl - sp_tl
        z_bot = z2_buf[h, half:, :]
        sp_bot = softplus2(z_bot)
        z2_buf[h, half:, :] = z_bot - sp_bot
        sp_top = jnp.concatenate([jnp.where(causal_q, sp_tl, 0.0), jnp.zeros_like(sp_tl)], axis=1)
        sp_bot = jnp.concatenate([sp_bot[:, :half], jnp.where(causal_q, sp_bot[:, half:], 0.0)],
                                 axis=1)
        return _dot(jnp.concatenate([sp_top, sp_bot], axis=0).astype(BF16), w)

    def weights(h, r, z2_buf, a_buf, masked):
        total = jnp.broadcast_to(r[:, K_BLOCK - 1:], (Q_TILE, LANES))
        if not masked:
            carry = carry_ref[h]
            nxt = carry + total
            carry_ref[h] = nxt
            suffix = jnp.where(last_key, 0.0, r)
            ah = jnp.exp2(z2_buf[h] + (suffix + jnp.concatenate([carry] * (K_BLOCK // LANES), axis=1)))
            a_buf[h] = ah.astype(BF16)
            return nxt
        carry_ref[h] = total
        a_tl = jnp.exp2(z2_buf[h, :half, :half] + r[:half, :half])
        a_bl = jnp.exp2(z2_buf[h, half:, :half] + r[half:, :half])
        a_br = jnp.exp2(z2_buf[h, half:, half:] + r[half:, half:])
        a_top = jnp.concatenate([jnp.where(causal_q, a_tl, 0.0), jnp.zeros_like(a_tl)], axis=1)
        a_bot = jnp.concatenate([a_bl, jnp.where(causal_q, a_br, 0.0)], axis=1)
        a_buf[h] = jnp.concatenate([a_top, a_bot], axis=0).astype(BF16)
        return total

    def block_step(kb, kb_prev, variant, masked):
        z2_cur, z2_next = (z2a_ref, z2b_ref) if variant == 0 else (z2b_ref, z2a_ref)
        a_cur, a_prev = (aa_ref, ab_ref) if variant == 0 else (ab_ref, aa_ref)
        if masked:
            for h in heads:
                logits(h, kb, z2_cur)
        r = {}
        worst = None
        for step in range(len(heads) + WEIGHTS_LAG):
            if step < len(heads):
                h = heads[step]
                logits(h, jnp.maximum(kb - 1, 0), z2_next)
                accumulate(h, a_prev, kb_prev)
                r[h] = suffix_sums(h, z2_cur, masked)
            if step >= WEIGHTS_LAG:
                h = heads[step - WEIGHTS_LAG]
                nxt = weights(h, r.pop(h), z2_cur, a_cur, masked)
                worst = nxt if worst is None else jnp.maximum(worst, nxt)
        return jnp.max(worst)

    def alternate(variant, fn):
        return lax.cond(variant == 0, lambda: fn(0), lambda: fn(1))

    def finalize(qstart):
        for g in range(n_groups):
            lanes = slice(g * LANES, (g + 1) * LANES)
            zg = z_ref[0, pl.ds(qstart, Q_TILE), lanes]
            acc = jnp.where(head0, acc_ref[2 * g], acc_ref[2 * g + 1])
            o_ref[0, pl.ds(qstart, Q_TILE), lanes] = (acc * _silu(zg)).astype(o_ref.dtype)
        acc_ref[...] = jnp.zeros_like(acc_ref)

    def q_tile(qi, state):
        variant, kb_prev = state
        qstart = pl.multiple_of(qi * Q_TILE, Q_TILE)

        def diagonal_step(v):
            for g in range(n_groups):
                q = q_ref[0, pl.ds(qstart, Q_TILE), g * LANES:(g + 1) * LANES]
                zero = jnp.zeros_like(q)
                qm_ref[2 * g] = jnp.where(head0, q, zero)
                qm_ref[2 * g + 1] = jnp.where(head0, zero, q)
            worst = block_step(qi, kb_prev, v, True)
            finalize(pl.multiple_of(jnp.maximum(qi - 1, 0) * Q_TILE, Q_TILE))
            return worst

        worst = alternate(variant, diagonal_step)

        def more_blocks(state):
            j, worst, _ = state
            return jnp.logical_and(j < qi, worst > NEGLIGIBLE_LOG2)

        def below_diagonal(state):
            j, _, variant = state
            kb = qi - 1 - j
            worst = alternate(variant, lambda v: block_step(kb, kb + 1, v, False))
            return j + 1, worst, 1 - variant

        n_below, _, variant = lax.while_loop(more_blocks, below_diagonal,
                                             (jnp.int32(0), worst, 1 - variant))
        return variant, qi - n_below

    aa_ref[...] = jnp.zeros_like(aa_ref)
    ab_ref[...] = jnp.zeros_like(ab_ref)
    acc_ref[...] = jnp.zeros_like(acc_ref)
    variant, kb_prev = lax.fori_loop(0, n_q, q_tile, (jnp.int32(0), jnp.int32(0)))

    def drain(v):
        a_prev = ab_ref if v == 0 else aa_ref
        for h in heads:
            accumulate(h, a_prev, kb_prev)

    alternate(variant, drain)
    finalize((n_q - 1) * Q_TILE)


def _attention(q, kv, z, *, groups_per_step=4):
    bsz, s, d = q.shape
    width = groups_per_step * LANES
    n_steps = d // width
    n_heads = 2 * groups_per_step
    grp = lambda b, g: (b, 0, g)
    return pl.pallas_call(
        _attn_kernel,
        grid=(bsz, n_steps),
        in_specs=[
            pl.BlockSpec((1, s, width), grp),
            pl.BlockSpec((1, s, width), grp),
            pl.BlockSpec((1, s, width), lambda b, g: (b, 0, n_steps + g)),
            pl.BlockSpec((1, s, width), grp),
            pl.BlockSpec((K_BLOCK, K_BLOCK), lambda b, g: (0, 0)),
        ],
        out_specs=pl.BlockSpec((1, s, width), grp),
        out_shape=jax.ShapeDtypeStruct((bsz, s, d), BF16),
        scratch_shapes=[
            pltpu.VMEM((n_heads, Q_TILE, LANES), BF16),
            pltpu.VMEM((n_heads, Q_TILE, LANES), F32),
            pltpu.VMEM((n_heads, Q_TILE, LANES), F32),
            pltpu.VMEM((n_heads, Q_TILE, K_BLOCK), F32),
            pltpu.VMEM((n_heads, Q_TILE, K_BLOCK), F32),
            pltpu.VMEM((n_heads, Q_TILE, K_BLOCK), BF16),
            pltpu.VMEM((n_heads, Q_TILE, K_BLOCK), BF16),
        ],
        compiler_params=pltpu.CompilerParams(
            dimension_semantics=("arbitrary", "arbitrary"),
            vmem_limit_bytes=40 * 1024 * 1024),
        name="stickbreak_attention",
    )(q, kv, kv, z, _suffix_weights())


def _out_kernel(o_ref, x_ref, mod_ref, wout_ref, g_ref, y_ref):
    gate = mod_ref[0][2:3]
    x = x_ref[0] + gate * _dot(o_ref[0], wout_ref[...].astype(BF16))
    y_ref[0] = (x * lax.rsqrt(jnp.mean(x * x, axis=-1, keepdims=True) + EPS)) * g_ref[...]


def _out_layer(o, x, modb, w_out, g, *, tm=1024):
    bsz, s, d = x.shape
    row = lambda b, i: (b, i, 0)
    return pl.pallas_call(
        _out_kernel,
        grid=(bsz, s // tm),
        in_specs=[
            pl.BlockSpec((1, tm, d), row),
            pl.BlockSpec((1, tm, d), row),
            pl.BlockSpec((1, 3, d), lambda b, i: (b, 0, 0)),
            pl.BlockSpec((d, d), lambda b, i: (0, 0)),
            pl.BlockSpec((1, d), lambda b, i: (0, 0)),
        ],
        out_specs=pl.BlockSpec((1, tm, d), row),
        out_shape=jax.ShapeDtypeStruct((bsz, s, d), F32),
        compiler_params=pltpu.CompilerParams(
            dimension_semantics=("arbitrary", "arbitrary"),
            vmem_limit_bytes=32 * 1024 * 1024),
        name="out_layer",
    )(o, x, modb, w_out, g)


def kernel(x, c, a_mod_w, a_mod_b, a_norm_g, a_w_in, a_conv_w, a_w_out,
           kv_mod_w, kv_mod_b, kv_norm_g, w_kv,
           b_mod_w, b_mod_b, b_norm_g, b_w_qz, b_w_out, final_norm_g):
    bsz, _, d = x.shape
    assert d == D_MODEL and a_mod_w.shape[0] == 1 and b_mod_w.shape[0] == 1

    mod = _modulation(c, [a_mod_w.reshape(d, 3 * d), kv_mod_w, b_mod_w.reshape(d, 3 * d)],
                      [a_mod_b, kv_mod_b, b_mod_b])
    mod_a = mod[:, :3 * d].reshape(bsz, 3, d)
    mod_kv = mod[:, 3 * d:5 * d].reshape(bsz, 2, d)
    mod_b = mod[:, 5 * d:].reshape(bsz, 3, d)

    x1 = _conv_layer(x, mod_a, a_norm_g, a_w_in[0], a_conv_w[0], a_w_out[0])
    q, kv, z = _projections(x1, mod_kv, mod_b, kv_norm_g.reshape(1, d), b_norm_g,
                            w_kv, b_w_qz[0])
    o = _attention(q, kv, z)
    return _out_layer(o, x1, mod_b, b_w_out[0], final_norm_g.reshape(1, d))
```

```python
import functools
import math

import jax
import jax.numpy as jnp
from jax import lax
from jax.experimental import pallas as pl
from jax.experimental.pallas import tpu as pltpu

D_MODEL = 1024
N_HEADS = 16
HEAD_DIM = D_MODEL // N_HEADS
CONV_K = 3
EPS = 1e-6
LOG2E = 1.4426950408889634

LANES = 128
SUBLANES = 8
MXU_DIM = 256

K_BLOCK = MXU_DIM
Q_TILE = K_BLOCK
NEGLIGIBLE_LOG2 = -135.0
WEIGHTS_LAG = 2

BF16 = jnp.bfloat16
F32 = jnp.float32


def _dot(a, b):
    return jnp.dot(a, b, preferred_element_type=F32)


def _silu(x):
    return x * jax.nn.sigmoid(x)


def _mod_kernel(c_ref, *refs, starts):
    n = len(starts) - 1
    w_refs, b_refs, o_ref = refs[:n], refs[n:2 * n], refs[2 * n]
    j = pl.program_id(0)
    s = _silu(c_ref[...]).astype(BF16)
    for i in range(n):
        @pl.when(jnp.logical_and(j >= starts[i], j < starts[i + 1]))
        def _(i=i):
            o_ref[...] = _dot(s, w_refs[i][...].astype(BF16)) + b_refs[i][...]


def _modulation(c, ws, bs, *, tn=1024):
    bsz, d = c.shape
    tiles = [w.shape[1] // tn for w in ws]
    starts = [sum(tiles[:i]) for i in range(len(ws) + 1)]

    def tile_of(i):
        return lambda j: (0, jnp.clip(j - starts[i], 0, tiles[i] - 1))

    return pl.pallas_call(
        functools.partial(_mod_kernel, starts=tuple(starts)),
        grid=(starts[-1],),
        in_specs=([pl.BlockSpec((bsz, d), lambda j: (0, 0))]
                  + [pl.BlockSpec((d, tn), tile_of(i)) for i in range(len(ws))]
                  + [pl.BlockSpec((1, tn), tile_of(i)) for i in range(len(ws))]),
        out_specs=pl.BlockSpec((bsz, tn), lambda j: (0, j)),
        out_shape=jax.ShapeDtypeStruct((bsz, starts[-1] * tn), F32),
        compiler_params=pltpu.CompilerParams(
            dimension_semantics=("arbitrary",),
            vmem_limit_bytes=40 * 1024 * 1024),
        name="modulation",
    )(c, *ws, *[b.reshape(1, -1) for b in bs])


def _conv_layer_kernel(x_ref, mod_ref, g_ref, win_ref, cw_ref, wout_ref, o_ref, carry_ref):
    tm = x_ref.shape[1]
    d = x_ref.shape[2]

    @pl.when(pl.program_id(1) == 0)
    def _():
        carry_ref[...] = jnp.zeros_like(carry_ref)

    x = x_ref[0]
    mod = mod_ref[0]
    shift, scale, gate = mod[0:1], mod[1:2], mod[2:3]
    rs = lax.rsqrt(jnp.mean(x * x, axis=-1, keepdims=True) + EPS)
    h = ((x * rs) * (g_ref[...] * (1.0 + scale)) + shift).astype(BF16)

    b_gate = _dot(h, win_ref[:, 0 * d:1 * d].astype(BF16))
    c_gate = _dot(h, win_ref[:, 1 * d:2 * d].astype(BF16))
    u = _dot(h, win_ref[:, 2 * d:3 * d].astype(BF16))
    z = _dot(h, win_ref[:, 3 * d:4 * d].astype(BF16))

    cu = c_gate * u
    ext = jnp.concatenate([carry_ref[...], cu], axis=0)
    prev1 = pltpu.roll(ext, 1, 0)[SUBLANES:]
    prev2 = pltpu.roll(ext, 2, 0)[SUBLANES:]
    carry_ref[...] = cu[tm - SUBLANES:]
    cw = cw_ref[...]
    conv = cw[0:1] * prev2 + cw[1:2] * prev1 + cw[2:3] * cu

    y = (b_gate * conv) * _silu(z)
    o_ref[0] = x + gate * _dot(y.astype(BF16), wout_ref[...].astype(BF16))


def _conv_layer(x, mod, g, w_in, conv_w, w_out, *, tm=1024):
    bsz, s, d = x.shape
    const = dict(pipeline_mode=pl.Buffered(1))
    return pl.pallas_call(
        _conv_layer_kernel,
        grid=(bsz, s // tm),
        in_specs=[
            pl.BlockSpec((1, tm, d), lambda b, i: (b, i, 0)),
            pl.BlockSpec((1, 3, d), lambda b, i: (b, 0, 0)),
            pl.BlockSpec((1, d), lambda b, i: (0, 0)),
            pl.BlockSpec((d, 4 * d), lambda b, i: (0, 0), **const),
            pl.BlockSpec((CONV_K, d), lambda b, i: (0, 0)),
            pl.BlockSpec((d, d), lambda b, i: (0, 0), **const),
        ],
        out_specs=pl.BlockSpec((1, tm, d), lambda b, i: (b, i, 0)),
        out_shape=jax.ShapeDtypeStruct((bsz, s, d), F32),
        scratch_shapes=[pltpu.VMEM((SUBLANES, d), F32)],
        compiler_params=pltpu.CompilerParams(
            dimension_semantics=("arbitrary", "arbitrary"),
            vmem_limit_bytes=56 * 1024 * 1024),
        name="conv_layer",
    )(x, mod, g, w_in, conv_w, w_out)


def _proj_kernel(x_ref, modkv_ref, modb_ref, gkv_ref, gb_ref, wkv_ref, wqz_ref,
                 q_ref, kv_ref, z_ref):
    d = x_ref.shape[2]
    x = x_ref[0]
    xn = x * lax.rsqrt(jnp.mean(x * x, axis=-1, keepdims=True) + EPS)
    modkv = modkv_ref[0]
    modb = modb_ref[0]
    hkv = (xn * (gkv_ref[...] * (1.0 + modkv[1:2])) + modkv[0:1]).astype(BF16)
    hq = (xn * (gb_ref[...] * (1.0 + modb[1:2])) + modb[0:1]).astype(BF16)
    kv_ref[0] = _dot(hkv, wkv_ref[...].astype(BF16)).astype(BF16)
    q_ref[0] = (_dot(hq, wqz_ref[:, :d].astype(BF16)) * (LOG2E / math.sqrt(HEAD_DIM))).astype(BF16)
    z_ref[0] = _dot(hq, wqz_ref[:, d:].astype(BF16))


def _projections(x, modkv, modb, gkv, gb, w_kv, w_qz, *, tm=1024):
    bsz, s, d = x.shape
    const = dict(pipeline_mode=pl.Buffered(1))
    row = lambda b, i: (b, i, 0)
    return pl.pallas_call(
        _proj_kernel,
        grid=(bsz, s // tm),
        in_specs=[
            pl.BlockSpec((1, tm, d), row),
            pl.BlockSpec((1, 2, d), lambda b, i: (b, 0, 0)),
            pl.BlockSpec((1, 3, d), lambda b, i: (b, 0, 0)),
            pl.BlockSpec((1, d), lambda b, i: (0, 0)),
            pl.BlockSpec((1, d), lambda b, i: (0, 0)),
            pl.BlockSpec((d, 2 * d), lambda b, i: (0, 0), **const),
            pl.BlockSpec((d, 2 * d), lambda b, i: (0, 0), **const),
        ],
        out_specs=[
            pl.BlockSpec((1, tm, d), row),
            pl.BlockSpec((1, tm, 2 * d), row),
            pl.BlockSpec((1, tm, d), row),
        ],
        out_shape=[
            jax.ShapeDtypeStruct((bsz, s, d), BF16),
            jax.ShapeDtypeStruct((bsz, s, 2 * d), BF16),
            jax.ShapeDtypeStruct((bsz, s, d), F32),
        ],
        compiler_params=pltpu.CompilerParams(
            dimension_semantics=("arbitrary", "arbitrary"),
            vmem_limit_bytes=56 * 1024 * 1024),
        name="projections",
    )(x, modkv, modb, gkv, gb, w_kv, w_qz)


def _suffix_weights():
    j = lax.broadcasted_iota(jnp.int32, (K_BLOCK, K_BLOCK), 0)
    s = lax.broadcasted_iota(jnp.int32, (K_BLOCK, K_BLOCK), 1)
    return jnp.where((j > s) | (s == K_BLOCK - 1), -1.0, 0.0).astype(BF16)


def _attn_kernel(q_ref, k_ref, v_ref, z_ref, w_ref, o_ref,
                 qm_ref, acc_ref, carry_ref, z2a_ref, z2b_ref, aa_ref, ab_ref):
    n_q = q_ref.shape[1] // Q_TILE
    n_groups = q_ref.shape[2] // LANES
    heads = tuple(range(2 * n_groups))
    lane = lax.broadcasted_iota(jnp.int32, (1, LANES), 1)
    head0 = lane < HEAD_DIM
    last_key = lax.broadcasted_iota(jnp.int32, (Q_TILE, K_BLOCK), 1) == K_BLOCK - 1
    quadrant = (Q_TILE // 2, K_BLOCK // 2)
    causal_q = (lax.broadcasted_iota(jnp.int32, quadrant, 1)
                < lax.broadcasted_iota(jnp.int32, quadrant, 0))
    w = w_ref[...]

    def block_start(kb):
        return pl.multiple_of(kb * K_BLOCK, K_BLOCK)

    def logits(h, kb, z2_buf):
        g = h // 2
        k_blk = k_ref[0, pl.ds(block_start(kb), K_BLOCK), g * LANES:(g + 1) * LANES]
        z2_buf[h] = lax.dot_general(qm_ref[h], k_blk, (((1,), (1,)), ((), ())),
                                    preferred_element_type=F32)

    def accumulate(h, a_buf, kb):
        g = h // 2
        v_blk = v_ref[0, pl.ds(block_start(kb), K_BLOCK), g * LANES:(g + 1) * LANES]
        acc_ref[h] += _dot(a_buf[h], v_blk)

    half = Q_TILE // 2

    def softplus2(z2):
        return jnp.maximum(z2, 0.0) + jnp.log(1.0 + jnp.exp2(-jnp.abs(z2))) * LOG2E

    def suffix_sums(h, z2_buf, masked):
        if not masked:
            z2 = z2_buf[h]
            sp = softplus2(z2)
            z2_buf[h] = z2 - sp
            return _dot(sp.astype(BF16), w)
        z_tl = z2_buf[h, :half, :half]
        sp_tl = softplus2(z_tl)
        z2_buf[h, :half, :half] = z_tl - sp_tl
        z_bot = z2_buf[h, half:, :]
        sp_bot = softplus2(z_bot)
        z2_buf[h, half:, :] = z_bot - sp_bot
        sp_top = jnp.concatenate([jnp.where(causal_q, sp_tl, 0.0), jnp.zeros_like(sp_tl)], axis=1)
        sp_bot = jnp.concatenate([sp_bot[:, :half], jnp.where(causal_q, sp_bot[:, half:], 0.0)],
                                 axis=1)
        return _dot(jnp.concatenate([sp_top, sp_bot], axis=0).astype(BF16), w)

    def weights(h, r, z2_buf, a_buf, masked):
        total = jnp.broadcast_to(r[:, K_BLOCK - 1:], (Q_TILE, LANES))
        if not masked:
            carry = carry_ref[h]
            nxt = carry + total
            carry_ref[h] = nxt
            suffix = jnp.where(last_key, 0.0, r)
            ah = jnp.exp2(z2_buf[h] + (suffix + jnp.concatenate([carry] * (K_BLOCK // LANES), axis=1)))
            a_buf[h] = ah.astype(BF16)
            return nxt
        carry_ref[h] = total
        a_tl = jnp.exp2(z2_buf[h, :half, :half] + r[:half, :half])
        a_bl = jnp.exp2(z2_buf[h, half:, :half] + r[half:, :half])
        a_br = jnp.exp2(z2_buf[h, half:, half:] + r[half:, half:])
        a_top = jnp.concatenate([jnp.where(causal_q, a_tl, 0.0), jnp.zeros_like(a_tl)], axis=1)
        a_bot = jnp.concatenate([a_bl, jnp.where(causal_q, a_br, 0.0)], axis=1)
        a_buf[h] = jnp.concatenate([a_top, a_bot], axis=0).astype(BF16)
        return total

    def block_step(kb, kb_prev, variant, masked):
        z2_cur, z2_next = (z2a_ref, z2b_ref) if variant == 0 else (z2b_ref, z2a_ref)
        a_cur, a_prev = (aa_ref, ab_ref) if variant == 0 else (ab_ref, aa_ref)
        if masked:
            for h in heads:
                logits(h, kb, z2_cur)
        r = {}
        worst = None
        for step in range(len(heads) + WEIGHTS_LAG):
            if step < len(heads):
                h = heads[step]
                logits(h, jnp.maximum(kb - 1, 0), z2_next)
                accumulate(h, a_prev, kb_prev)
                r[h] = suffix_sums(h, z2_cur, masked)
            if step >= WEIGHTS_LAG:
                h = heads[step - WEIGHTS_LAG]
                nxt = weights(h, r.pop(h), z2_cur, a_cur, masked)
                if not masked:
                    worst = nxt if worst is None else jnp.maximum(worst, nxt)
        return None if masked else jnp.max(worst)

    def alternate(variant, fn):
        return lax.cond(variant == 0, lambda: fn(0), lambda: fn(1))

    def finalize(qstart):
        for g in range(n_groups):
            lanes = slice(g * LANES, (g + 1) * LANES)
            zg = z_ref[0, pl.ds(qstart, Q_TILE), lanes]
            acc = jnp.where(head0, acc_ref[2 * g], acc_ref[2 * g + 1])
            o_ref[0, pl.ds(qstart, Q_TILE), lanes] = (acc * _silu(zg)).astype(o_ref.dtype)
        acc_ref[...] = jnp.zeros_like(acc_ref)

    def q_tile(qi, state):
        variant, kb_prev = state
        qstart = pl.multiple_of(qi * Q_TILE, Q_TILE)

        def diagonal_step(v):
            for g in range(n_groups):
                q = q_ref[0, pl.ds(qstart, Q_TILE), g * LANES:(g + 1) * LANES]
                zero = jnp.zeros_like(q)
                qm_ref[2 * g] = jnp.where(head0, q, zero)
                qm_ref[2 * g + 1] = jnp.where(head0, zero, q)
            block_step(qi, kb_prev, v, True)
            finalize(pl.multiple_of(jnp.maximum(qi - 1, 0) * Q_TILE, Q_TILE))

        alternate(variant, diagonal_step)

        def more_blocks(state):
            j, worst, _ = state
            return jnp.logical_and(j < qi, worst > NEGLIGIBLE_LOG2)

        def below_diagonal(state):
            j, _, variant = state
            kb = qi - 1 - j
            worst = alternate(variant, lambda v: block_step(kb, kb + 1, v, False))
            return j + 1, worst, 1 - variant

        n_below, _, variant = lax.while_loop(more_blocks, below_diagonal,
                                             (jnp.int32(0), jnp.float32(0.0), 1 - variant))
        return variant, qi - n_below

    aa_ref[...] = jnp.zeros_like(aa_ref)
    ab_ref[...] = jnp.zeros_like(ab_ref)
    acc_ref[...] = jnp.zeros_like(acc_ref)
    variant, kb_prev = lax.fori_loop(0, n_q, q_tile, (jnp.int32(0), jnp.int32(0)))

    def drain(v):
        a_prev = ab_ref if v == 0 else aa_ref
        for h in heads:
            accumulate(h, a_prev, kb_prev)

    alternate(variant, drain)
    finalize((n_q - 1) * Q_TILE)


def _attention(q, kv, z, *, groups_per_step=4):
    bsz, s, d = q.shape
    width = groups_per_step * LANES
    n_steps = d // width
    n_heads = 2 * groups_per_step
    grp = lambda b, g: (b, 0, g)
    return pl.pallas_call(
        _attn_kernel,
        grid=(bsz, n_steps),
        in_specs=[
            pl.BlockSpec((1, s, width), grp),
            pl.BlockSpec((1, s, width), grp),
            pl.BlockSpec((1, s, width), lambda b, g: (b, 0, n_steps + g)),
            pl.BlockSpec((1, s, width), grp),
            pl.BlockSpec((K_BLOCK, K_BLOCK), lambda b, g: (0, 0)),
        ],
        out_specs=pl.BlockSpec((1, s, width), grp),
        out_shape=jax.ShapeDtypeStruct((bsz, s, d), BF16),
        scratch_shapes=[
            pltpu.VMEM((n_heads, Q_TILE, LANES), BF16),
            pltpu.VMEM((n_heads, Q_TILE, LANES), F32),
            pltpu.VMEM((n_heads, Q_TILE, LANES), F32),
            pltpu.VMEM((n_heads, Q_TILE, K_BLOCK), F32),
            pltpu.VMEM((n_heads, Q_TILE, K_BLOCK), F32),
            pltpu.VMEM((n_heads, Q_TILE, K_BLOCK), BF16),
            pltpu.VMEM((n_heads, Q_TILE, K_BLOCK), BF16),
        ],
        compiler_params=pltpu.CompilerParams(
            dimension_semantics=("arbitrary", "arbitrary"),
            vmem_limit_bytes=40 * 1024 * 1024),
        name="stickbreak_attention",
    )(q, kv, kv, z, _suffix_weights())


def _out_kernel(o_ref, x_ref, mod_ref, wout_ref, g_ref, y_ref):
    gate = mod_ref[0][2:3]
    x = x_ref[0] + gate * _dot(o_ref[0], wout_ref[...].astype(BF16))
    y_ref[0] = (x * lax.rsqrt(jnp.mean(x * x, axis=-1, keepdims=True) + EPS)) * g_ref[...]


def _out_layer(o, x, modb, w_out, g, *, tm=1024):
    bsz, s, d = x.shape
    row = lambda b, i: (b, i, 0)
    return pl.pallas_call(
        _out_kernel,
        grid=(bsz, s // tm),
        in_specs=[
            pl.BlockSpec((1, tm, d), row),
            pl.BlockSpec((1, tm, d), row),
            pl.BlockSpec((1, 3, d), lambda b, i: (b, 0, 0)),
            pl.BlockSpec((d, d), lambda b, i: (0, 0)),
            pl.BlockSpec((1, d), lambda b, i: (0, 0)),
        ],
        out_specs=pl.BlockSpec((1, tm, d), row),
        out_shape=jax.ShapeDtypeStruct((bsz, s, d), F32),
        compiler_params=pltpu.CompilerParams(
            dimension_semantics=("arbitrary", "arbitrary"),
            vmem_limit_bytes=32 * 1024 * 1024),
        name="out_layer",
    )(o, x, modb, w_out, g)


def kernel(x, c, a_mod_w, a_mod_b, a_norm_g, a_w_in, a_conv_w, a_w_out,
           kv_mod_w, kv_mod_b, kv_norm_g, w_kv,
           b_mod_w, b_mod_b, b_norm_g, b_w_qz, b_w_out, final_norm_g):
    bsz, _, d = x.shape
    assert d == D_MODEL and a_mod_w.shape[0] == 1 and b_mod_w.shape[0] == 1

    mod = _modulation(c, [a_mod_w.reshape(d, 3 * d), kv_mod_w, b_mod_w.reshape(d, 3 * d)],
                      [a_mod_b, kv_mod_b, b_mod_b])
    mod_a = mod[:, :3 * d].reshape(bsz, 3, d)
    mod_kv = mod[:, 3 * d:5 * d].reshape(bsz, 2, d)
    mod_b = mod[:, 5 * d:].reshape(bsz, 3, d)

    x1 = _conv_layer(x, mod_a, a_norm_g, a_w_in[0], a_conv_w[0], a_w_out[0])
    q, kv, z = _projections(x1, mod_kv, mod_b, kv_norm_g.reshape(1, d), b_norm_g,
                            w_kv, b_w_qz[0])
    o = _attention(q, kv, z)
    return _out_layer(o, x1, mod_b, b_w_out[0], final_norm_g.reshape(1, d))
```

```python
import functools
import math

import jax
import jax.numpy as jnp
from jax import lax
from jax.experimental import pallas as pl
from jax.experimental.pallas import tpu as pltpu

D_MODEL = 1024
N_HEADS = 16
HEAD_DIM = D_MODEL // N_HEADS
CONV_K = 3
EPS = 1e-6
LOG2E = 1.4426950408889634

LANES = 128
SUBLANES = 8
MXU_DIM = 256

K_BLOCK = MXU_DIM
Q_TILE = K_BLOCK
NEGLIGIBLE_LOG2 = -135.0
WEIGHTS_LAG = 2
LOGITS_LEAD = 2

BF16 = jnp.bfloat16
F32 = jnp.float32


def _dot(a, b):
    return jnp.dot(a, b, preferred_element_type=F32)


def _silu(x):
    return x * jax.nn.sigmoid(x)


def _mod_kernel(c_ref, *refs, starts):
    n = len(starts) - 1
    w_refs, b_refs, o_ref = refs[:n], refs[n:2 * n], refs[2 * n]
    j = pl.program_id(0)
    s = _silu(c_ref[...]).astype(BF16)
    for i in range(n):
        @pl.when(jnp.logical_and(j >= starts[i], j < starts[i + 1]))
        def _(i=i):
            o_ref[...] = _dot(s, w_refs[i][...].astype(BF16)) + b_refs[i][...]


def _modulation(c, ws, bs, *, tn=1024):
    bsz, d = c.shape
    tiles = [w.shape[1] // tn for w in ws]
    starts = [sum(tiles[:i]) for i in range(len(ws) + 1)]

    def tile_of(i):
        return lambda j: (0, jnp.clip(j - starts[i], 0, tiles[i] - 1))

    return pl.pallas_call(
        functools.partial(_mod_kernel, starts=tuple(starts)),
        grid=(starts[-1],),
        in_specs=([pl.BlockSpec((bsz, d), lambda j: (0, 0))]
                  + [pl.BlockSpec((d, tn), tile_of(i)) for i in range(len(ws))]
                  + [pl.BlockSpec((1, tn), tile_of(i)) for i in range(len(ws))]),
        out_specs=pl.BlockSpec((bsz, tn), lambda j: (0, j)),
        out_shape=jax.ShapeDtypeStruct((bsz, starts[-1] * tn), F32),
        compiler_params=pltpu.CompilerParams(
            dimension_semantics=("arbitrary",),
            vmem_limit_bytes=40 * 1024 * 1024),
        name="modulation",
    )(c, *ws, *[b.reshape(1, -1) for b in bs])


def _conv_layer_kernel(x_ref, mod_ref, g_ref, win_ref, cw_ref, wout_ref, o_ref, carry_ref):
    tm = x_ref.shape[1]
    d = x_ref.shape[2]

    @pl.when(pl.program_id(1) == 0)
    def _():
        carry_ref[...] = jnp.zeros_like(carry_ref)

    x = x_ref[0]
    mod = mod_ref[0]
    shift, scale, gate = mod[0:1], mod[1:2], mod[2:3]
    rs = lax.rsqrt(jnp.mean(x * x, axis=-1, keepdims=True) + EPS)
    h = ((x * rs) * (g_ref[...] * (1.0 + scale)) + shift).astype(BF16)

    b_gate = _dot(h, win_ref[:, 0 * d:1 * d].astype(BF16))
    c_gate = _dot(h, win_ref[:, 1 * d:2 * d].astype(BF16))
    u = _dot(h, win_ref[:, 2 * d:3 * d].astype(BF16))
    z = _dot(h, win_ref[:, 3 * d:4 * d].astype(BF16))

    cu = c_gate * u
    ext = jnp.concatenate([carry_ref[...], cu], axis=0)
    prev1 = pltpu.roll(ext, 1, 0)[SUBLANES:]
    prev2 = pltpu.roll(ext, 2, 0)[SUBLANES:]
    carry_ref[...] = cu[tm - SUBLANES:]
    cw = cw_ref[...]
    conv = cw[0:1] * prev2 + cw[1:2] * prev1 + cw[2:3] * cu

    y = (b_gate * conv) * _silu(z)
    o_ref[0] = x + gate * _dot(y.astype(BF16), wout_ref[...].astype(BF16))


def _conv_layer(x, mod, g, w_in, conv_w, w_out, *, tm=1024):
    bsz, s, d = x.shape
    const = dict(pipeline_mode=pl.Buffered(1))
    return pl.pallas_call(
        _conv_layer_kernel,
        grid=(bsz, s // tm),
        in_specs=[
            pl.BlockSpec((1, tm, d), lambda b, i: (b, i, 0)),
            pl.BlockSpec((1, 3, d), lambda b, i: (b, 0, 0)),
            pl.BlockSpec((1, d), lambda b, i: (0, 0)),
            pl.BlockSpec((d, 4 * d), lambda b, i: (0, 0), **const),
            pl.BlockSpec((CONV_K, d), lambda b, i: (0, 0)),
            pl.BlockSpec((d, d), lambda b, i: (0, 0), **const),
        ],
        out_specs=pl.BlockSpec((1, tm, d), lambda b, i: (b, i, 0)),
        out_shape=jax.ShapeDtypeStruct((bsz, s, d), F32),
        scratch_shapes=[pltpu.VMEM((SUBLANES, d), F32)],
        compiler_params=pltpu.CompilerParams(
            dimension_semantics=("arbitrary", "arbitrary"),
            vmem_limit_bytes=56 * 1024 * 1024),
        name="conv_layer",
    )(x, mod, g, w_in, conv_w, w_out)


def _proj_kernel(x_ref, modkv_ref, modb_ref, gkv_ref, gb_ref, wkv_ref, wqz_ref,
                 q_ref, kv_ref, z_ref):
    d = x_ref.shape[2]
    x = x_ref[0]
    xn = x * lax.rsqrt(jnp.mean(x * x, axis=-1, keepdims=True) + EPS)
    modkv = modkv_ref[0]
    modb = modb_ref[0]
    hkv = (xn * (gkv_ref[...] * (1.0 + modkv[1:2])) + modkv[0:1]).astype(BF16)
    hq = (xn * (gb_ref[...] * (1.0 + modb[1:2])) + modb[0:1]).astype(BF16)
    kv_ref[0] = _dot(hkv, wkv_ref[...].astype(BF16)).astype(BF16)
    q_ref[0] = (_dot(hq, wqz_ref[:, :d].astype(BF16)) * (LOG2E / math.sqrt(HEAD_DIM))).astype(BF16)
    z_ref[0] = _dot(hq, wqz_ref[:, d:].astype(BF16))


def _projections(x, modkv, modb, gkv, gb, w_kv, w_qz, *, tm=1024):
    bsz, s, d = x.shape
    const = dict(pipeline_mode=pl.Buffered(1))
    row = lambda b, i: (b, i, 0)
    return pl.pallas_call(
        _proj_kernel,
        grid=(bsz, s // tm),
        in_specs=[
            pl.BlockSpec((1, tm, d), row),
            pl.BlockSpec((1, 2, d), lambda b, i: (b, 0, 0)),
            pl.BlockSpec((1, 3, d), lambda b, i: (b, 0, 0)),
            pl.BlockSpec((1, d), lambda b, i: (0, 0)),
            pl.BlockSpec((1, d), lambda b, i: (0, 0)),
            pl.BlockSpec((d, 2 * d), lambda b, i: (0, 0), **const),
            pl.BlockSpec((d, 2 * d), lambda b, i: (0, 0), **const),
        ],
        out_specs=[
            pl.BlockSpec((1, tm, d), row),
            pl.BlockSpec((1, tm, 2 * d), row),
            pl.BlockSpec((1, tm, d), row),
        ],
        out_shape=[
            jax.ShapeDtypeStruct((bsz, s, d), BF16),
            jax.ShapeDtypeStruct((bsz, s, 2 * d), BF16),
            jax.ShapeDtypeStruct((bsz, s, d), F32),
        ],
        compiler_params=pltpu.CompilerParams(
            dimension_semantics=("arbitrary", "arbitrary"),
            vmem_limit_bytes=56 * 1024 * 1024),
        name="projections",
    )(x, modkv, modb, gkv, gb, w_kv, w_qz)


def _suffix_weights():
    j = lax.broadcasted_iota(jnp.int32, (K_BLOCK, K_BLOCK), 0)
    s = lax.broadcasted_iota(jnp.int32, (K_BLOCK, K_BLOCK), 1)
    return jnp.where((j > s) | (s == K_BLOCK - 1), -1.0, 0.0).astype(BF16)


def _attn_kernel(q_ref, k_ref, v_ref, z_ref, w_ref, o_ref,
                 qm_ref, acc_ref, carry_ref, z2a_ref, z2b_ref, aa_ref, ab_ref):
    n_q = q_ref.shape[1] // Q_TILE
    n_groups = q_ref.shape[2] // LANES
    heads = tuple(range(2 * n_groups))
    lane = lax.broadcasted_iota(jnp.int32, (1, LANES), 1)
    head0 = lane < HEAD_DIM
    last_key = lax.broadcasted_iota(jnp.int32, (Q_TILE, K_BLOCK), 1) == K_BLOCK - 1
    quadrant = (Q_TILE // 2, K_BLOCK // 2)
    causal_q = (lax.broadcasted_iota(jnp.int32, quadrant, 1)
                < lax.broadcasted_iota(jnp.int32, quadrant, 0))
    w = w_ref[...]

    def block_start(kb):
        return pl.multiple_of(kb * K_BLOCK, K_BLOCK)

    def logits(h, kb, z2_buf):
        g = h // 2
        k_blk = k_ref[0, pl.ds(block_start(kb), K_BLOCK), g * LANES:(g + 1) * LANES]
        z2_buf[h] = lax.dot_general(qm_ref[h], k_blk, (((1,), (1,)), ((), ())),
                                    preferred_element_type=F32)

    def accumulate(h, a_buf, kb):
        g = h // 2
        v_blk = v_ref[0, pl.ds(block_start(kb), K_BLOCK), g * LANES:(g + 1) * LANES]
        acc_ref[h] += _dot(a_buf[h], v_blk)

    half = Q_TILE // 2

    def softplus2(z2):
        return jnp.maximum(z2, 0.0) + jnp.log(1.0 + jnp.exp2(-jnp.abs(z2))) * LOG2E

    def suffix_sums(h, z2_buf, masked):
        if not masked:
            z2 = z2_buf[h]
            sp = softplus2(z2)
            z2_buf[h] = z2 - sp
            return _dot(sp.astype(BF16), w)
        z_tl = z2_buf[h, :half, :half]
        sp_tl = softplus2(z_tl)
        z2_buf[h, :half, :half] = z_tl - sp_tl
        z_bot = z2_buf[h, half:, :]
        sp_bot = softplus2(z_bot)
        z2_buf[h, half:, :] = z_bot - sp_bot
        sp_top = jnp.concatenate([jnp.where(causal_q, sp_tl, 0.0), jnp.zeros_like(sp_tl)], axis=1)
        sp_bot = jnp.concatenate([sp_bot[:, :half], jnp.where(causal_q, sp_bot[:, half:], 0.0)],
                                 axis=1)
        return _dot(jnp.concatenate([sp_top, sp_bot], axis=0).astype(BF16), w)

    def weights(h, r, z2_buf, a_buf, masked):
        total = jnp.broadcast_to(r[:, K_BLOCK - 1:], (Q_TILE, LANES))
        if not masked:
            carry = carry_ref[h]
            nxt = carry + total
            carry_ref[h] = nxt
            suffix = jnp.where(last_key, 0.0, r)
            ah = jnp.exp2(z2_buf[h] + (suffix + jnp.concatenate([carry] * (K_BLOCK // LANES), axis=1)))
            a_buf[h] = ah.astype(BF16)
            return nxt
        carry_ref[h] = total
        a_tl = jnp.exp2(z2_buf[h, :half, :half] + r[:half, :half])
        a_bl = jnp.exp2(z2_buf[h, half:, :half] + r[half:, :half])
        a_br = jnp.exp2(z2_buf[h, half:, half:] + r[half:, half:])
        a_top = jnp.concatenate([jnp.where(causal_q, a_tl, 0.0), jnp.zeros_like(a_tl)], axis=1)
        a_bot = jnp.concatenate([a_bl, jnp.where(causal_q, a_br, 0.0)], axis=1)
        a_buf[h] = jnp.concatenate([a_top, a_bot], axis=0).astype(BF16)
        return total

    def block_step(kb, kb_prev, variant, masked):
        z2_cur, z2_next = (z2a_ref, z2b_ref) if variant == 0 else (z2b_ref, z2a_ref)
        a_cur, a_prev = (aa_ref, ab_ref) if variant == 0 else (ab_ref, aa_ref)
        if masked:
            for h in heads[:LOGITS_LEAD]:
                logits(h, kb, z2_cur)
        r = {}
        worst = None
        for step in range(len(heads) + WEIGHTS_LAG):
            if masked and step + LOGITS_LEAD < len(heads):
                logits(heads[step + LOGITS_LEAD], kb, z2_cur)
            if step < len(heads):
                h = heads[step]
                r[h] = suffix_sums(h, z2_cur, masked)
                logits(h, jnp.maximum(kb - 1, 0), z2_next)
                accumulate(h, a_prev, kb_prev)
            if step >= WEIGHTS_LAG:
                h = heads[step - WEIGHTS_LAG]
                nxt = weights(h, r.pop(h), z2_cur, a_cur, masked)
                if not masked:
                    worst = nxt if worst is None else jnp.maximum(worst, nxt)
        return None if masked else jnp.max(worst)

    def alternate(variant, fn):
        return lax.cond(variant == 0, lambda: fn(0), lambda: fn(1))

    def finalize(qstart):
        for g in range(n_groups):
            lanes = slice(g * LANES, (g + 1) * LANES)
            zg = z_ref[0, pl.ds(qstart, Q_TILE), lanes]
            acc = jnp.where(head0, acc_ref[2 * g], acc_ref[2 * g + 1])
            o_ref[0, pl.ds(qstart, Q_TILE), lanes] = (acc * _silu(zg)).astype(o_ref.dtype)
        acc_ref[...] = jnp.zeros_like(acc_ref)

    def q_tile(qi, state):
        variant, kb_prev = state
        qstart = pl.multiple_of(qi * Q_TILE, Q_TILE)

        def diagonal_step(v):
            for g in range(n_groups):
                q = q_ref[0, pl.ds(qstart, Q_TILE), g * LANES:(g + 1) * LANES]
                zero = jnp.zeros_like(q)
                qm_ref[2 * g] = jnp.where(head0, q, zero)
                qm_ref[2 * g + 1] = jnp.where(head0, zero, q)
            block_step(qi, kb_prev, v, True)
            finalize(pl.multiple_of(jnp.maximum(qi - 1, 0) * Q_TILE, Q_TILE))

        alternate(variant, diagonal_step)

        def more_blocks(state):
            j, worst, _ = state
            return jnp.logical_and(j < qi, worst > NEGLIGIBLE_LOG2)

        def below_diagonal(state):
            j, _, variant = state
            kb = qi - 1 - j
            worst = alternate(variant, lambda v: block_step(kb, kb + 1, v, False))
            return j + 1, worst, 1 - variant

        n_below, _, variant = lax.while_loop(more_blocks, below_diagonal,
                                             (jnp.int32(0), jnp.float32(0.0), 1 - variant))
        return variant, qi - n_below

    aa_ref[...] = jnp.zeros_like(aa_ref)
    ab_ref[...] = jnp.zeros_like(ab_ref)
    acc_ref[...] = jnp.zeros_like(acc_ref)
    variant, kb_prev = lax.fori_loop(0, n_q, q_tile, (jnp.int32(0), jnp.int32(0)))

    def drain(v):
        a_prev = ab_ref if v == 0 else aa_ref
        for h in heads:
            accumulate(h, a_prev, kb_prev)

    alternate(variant, drain)
    finalize((n_q - 1) * Q_TILE)


def _attention(q, kv, z, *, groups_per_step=4):
    bsz, s, d = q.shape
    width = groups_per_step * LANES
    n_steps = d // width
    n_heads = 2 * groups_per_step
    grp = lambda b, g: (b, 0, g)
    return pl.pallas_call(
        _attn_kernel,
        grid=(bsz, n_steps),
        in_specs=[
            pl.BlockSpec((1, s, width), grp),
            pl.BlockSpec((1, s, width), grp),
            pl.BlockSpec((1, s, width), lambda b, g: (b, 0, n_steps + g)),
            pl.BlockSpec((1, s, width), grp),
            pl.BlockSpec((K_BLOCK, K_BLOCK), lambda b, g: (0, 0)),
        ],
        out_specs=pl.BlockSpec((1, s, width), grp),
        out_shape=jax.ShapeDtypeStruct((bsz, s, d), BF16),
        scratch_shapes=[
            pltpu.VMEM((n_heads, Q_TILE, LANES), BF16),
            pltpu.VMEM((n_heads, Q_TILE, LANES), F32),
            pltpu.VMEM((n_heads, Q_TILE, LANES), F32),
            pltpu.VMEM((n_heads, Q_TILE, K_BLOCK), F32),
            pltpu.VMEM((n_heads, Q_TILE, K_BLOCK), F32),
            pltpu.VMEM((n_heads, Q_TILE, K_BLOCK), BF16),
            pltpu.VMEM((n_heads, Q_TILE, K_BLOCK), BF16),
        ],
        compiler_params=pltpu.CompilerParams(
            dimension_semantics=("arbitrary", "arbitrary"),
            vmem_limit_bytes=40 * 1024 * 1024),
        name="stickbreak_attention",
    )(q, kv, kv, z, _suffix_weights())


def _out_kernel(o_ref, x_ref, mod_ref, wout_ref, g_ref, y_ref):
    gate = mod_ref[0][2:3]
    x = x_ref[0] + gate * _dot(o_ref[0], wout_ref[...].astype(BF16))
    y_ref[0] = (x * lax.rsqrt(jnp.mean(x * x, axis=-1, keepdims=True) + EPS)) * g_ref[...]


def _out_layer(o, x, modb, w_out, g, *, tm=1024):
    bsz, s, d = x.shape
    row = lambda b, i: (b, i, 0)
    return pl.pallas_call(
        _out_kernel,
        grid=(bsz, s // tm),
        in_specs=[
            pl.BlockSpec((1, tm, d), row),
            pl.BlockSpec((1, tm, d), row),
            pl.BlockSpec((1, 3, d), lambda b, i: (b, 0, 0)),
            pl.BlockSpec((d, d), lambda b, i: (0, 0)),
            pl.BlockSpec((1, d), lambda b, i: (0, 0)),
        ],
        out_specs=pl.BlockSpec((1, tm, d), row),
        out_shape=jax.ShapeDtypeStruct((bsz, s, d), F32),
        compiler_params=pltpu.CompilerParams(
            dimension_semantics=("arbitrary", "arbitrary"),
            vmem_limit_bytes=32 * 1024 * 1024),
        name="out_layer",
    )(o, x, modb, w_out, g)


def kernel(x, c, a_mod_w, a_mod_b, a_norm_g, a_w_in, a_conv_w, a_w_out,
           kv_mod_w, kv_mod_b, kv_norm_g, w_kv,
           b_mod_w, b_mod_b, b_norm_g, b_w_qz, b_w_out, final_norm_g):
    bsz, _, d = x.shape
    assert d == D_MODEL and a_mod_w.shape[0] == 1 and b_mod_w.shape[0] == 1

    mod = _modulation(c, [a_mod_w.reshape(d, 3 * d), kv_mod_w, b_mod_w.reshape(d, 3 * d)],
                      [a_mod_b, kv_mod_b, b_mod_b])
    mod_a = mod[:, :3 * d].reshape(bsz, 3, d)
    mod_kv = mod[:, 3 * d:5 * d].reshape(bsz, 2, d)
    mod_b = mod[:, 5 * d:].reshape(bsz, 3, d)

    x1 = _conv_layer(x, mod_a, a_norm_g, a_w_in[0], a_conv_w[0], a_w_out[0])
    q, kv, z = _projections(x1, mod_kv, mod_b, kv_norm_g.reshape(1, d), b_norm_g,
                            w_kv, b_w_qz[0])
    o = _attention(q, kv, z)
    return _out_layer(o, x1, mod_b, b_w_out[0], final_norm_g.reshape(1, d))
```

```python
import functools
import math

import jax
import jax.numpy as jnp
from jax import lax
from jax.experimental import pallas as pl
from jax.experimental.pallas import tpu as pltpu

D_MODEL = 1024
N_HEADS = 16
HEAD_DIM = D_MODEL // N_HEADS
CONV_K = 3
EPS = 1e-6
LOG2E = 1.4426950408889634

LANES = 128
SUBLANES = 8
MXU_DIM = 256

K_BLOCK = MXU_DIM
Q_TILE = K_BLOCK
NEGLIGIBLE_LOG2 = -135.0
WEIGHTS_LAG = 2
CONV_ROW_CHUNKS = 4
LOGITS_LEAD = 2

BF16 = jnp.bfloat16
F32 = jnp.float32


def _dot(a, b):
    return jnp.dot(a, b, preferred_element_type=F32)


def _silu(x):
    return x * jax.nn.sigmoid(x)


def _mod_kernel(c_ref, *refs, starts):
    n = len(starts) - 1
    w_refs, b_refs, o_ref = refs[:n], refs[n:2 * n], refs[2 * n]
    j = pl.program_id(0)
    s = _silu(c_ref[...]).astype(BF16)
    for i in range(n):
        @pl.when(jnp.logical_and(j >= starts[i], j < starts[i + 1]))
        def _(i=i):
            o_ref[...] = _dot(s, w_refs[i][...].astype(BF16)) + b_refs[i][...]


def _modulation(c, ws, bs, *, tn=1024):
    bsz, d = c.shape
    tiles = [w.shape[1] // tn for w in ws]
    starts = [sum(tiles[:i]) for i in range(len(ws) + 1)]

    def tile_of(i):
        return lambda j: (0, jnp.clip(j - starts[i], 0, tiles[i] - 1))

    return pl.pallas_call(
        functools.partial(_mod_kernel, starts=tuple(starts)),
        grid=(starts[-1],),
        in_specs=([pl.BlockSpec((bsz, d), lambda j: (0, 0))]
                  + [pl.BlockSpec((d, tn), tile_of(i)) for i in range(len(ws))]
                  + [pl.BlockSpec((1, tn), tile_of(i)) for i in range(len(ws))]),
        out_specs=pl.BlockSpec((bsz, tn), lambda j: (0, j)),
        out_shape=jax.ShapeDtypeStruct((bsz, starts[-1] * tn), F32),
        compiler_params=pltpu.CompilerParams(
            dimension_semantics=("arbitrary",),
            vmem_limit_bytes=40 * 1024 * 1024),
        name="modulation",
    )(c, *ws, *[b.reshape(1, -1) for b in bs])


def _conv_layer_kernel(x_ref, mod_ref, g_ref, win_ref, cw_ref, wout_ref, o_ref, carry_ref):
    tm = x_ref.shape[1]
    d = x_ref.shape[2]

    @pl.when(pl.program_id(1) == 0)
    def _():
        carry_ref[...] = jnp.zeros_like(carry_ref)

    mod = mod_ref[0]
    shift, scale, gate = mod[0:1], mod[1:2], mod[2:3]
    gain = g_ref[...] * (1.0 + scale)
    cw = cw_ref[...]
    w_in = [win_ref[:, p * d:(p + 1) * d].astype(BF16) for p in range(4)]
    w_out = wout_ref[...].astype(BF16)

    rows = tm // CONV_ROW_CHUNKS
    tail = carry_ref[...]
    for r0 in range(0, tm, rows):
        x = x_ref[0, r0:r0 + rows, :]
        rs = lax.rsqrt(jnp.mean(x * x, axis=-1, keepdims=True) + EPS)
        h = ((x * rs) * gain + shift).astype(BF16)
        b_gate, c_gate, u, z = (_dot(h, w) for w in w_in)

        cu = c_gate * u
        ext = jnp.concatenate([tail, cu], axis=0)
        prev1 = pltpu.roll(ext, 1, 0)[SUBLANES:]
        prev2 = pltpu.roll(ext, 2, 0)[SUBLANES:]
        tail = cu[rows - SUBLANES:]
        conv = cw[0:1] * prev2 + cw[1:2] * prev1 + cw[2:3] * cu

        y = (b_gate * conv) * _silu(z)
        o_ref[0, r0:r0 + rows, :] = x + gate * _dot(y.astype(BF16), w_out)
    carry_ref[...] = tail


def _conv_layer(x, mod, g, w_in, conv_w, w_out, *, tm=1024):
    bsz, s, d = x.shape
    const = dict(pipeline_mode=pl.Buffered(1))
    return pl.pallas_call(
        _conv_layer_kernel,
        grid=(bsz, s // tm),
        in_specs=[
            pl.BlockSpec((1, tm, d), lambda b, i: (b, i, 0)),
            pl.BlockSpec((1, 3, d), lambda b, i: (b, 0, 0)),
            pl.BlockSpec((1, d), lambda b, i: (0, 0)),
            pl.BlockSpec((d, 4 * d), lambda b, i: (0, 0), **const),
            pl.BlockSpec((CONV_K, d), lambda b, i: (0, 0)),
            pl.BlockSpec((d, d), lambda b, i: (0, 0), **const),
        ],
        out_specs=pl.BlockSpec((1, tm, d), lambda b, i: (b, i, 0)),
        out_shape=jax.ShapeDtypeStruct((bsz, s, d), F32),
        scratch_shapes=[pltpu.VMEM((SUBLANES, d), F32)],
        compiler_params=pltpu.CompilerParams(
            dimension_semantics=("arbitrary", "arbitrary"),
            vmem_limit_bytes=56 * 1024 * 1024),
        name="conv_layer",
    )(x, mod, g, w_in, conv_w, w_out)


def _proj_kernel(x_ref, modkv_ref, modb_ref, gkv_ref, gb_ref, wkv_ref, wqz_ref,
                 q_ref, kv_ref, z_ref):
    d = x_ref.shape[2]
    x = x_ref[0]
    xn = x * lax.rsqrt(jnp.mean(x * x, axis=-1, keepdims=True) + EPS)
    modkv = modkv_ref[0]
    modb = modb_ref[0]
    hkv = (xn * (gkv_ref[...] * (1.0 + modkv[1:2])) + modkv[0:1]).astype(BF16)
    hq = (xn * (gb_ref[...] * (1.0 + modb[1:2])) + modb[0:1]).astype(BF16)
    kv_ref[0] = _dot(hkv, wkv_ref[...].astype(BF16)).astype(BF16)
    q_ref[0] = (_dot(hq, wqz_ref[:, :d].astype(BF16)) * (LOG2E / math.sqrt(HEAD_DIM))).astype(BF16)
    z_ref[0] = _dot(hq, wqz_ref[:, d:].astype(BF16))


def _projections(x, modkv, modb, gkv, gb, w_kv, w_qz, *, tm=1024):
    bsz, s, d = x.shape
    const = dict(pipeline_mode=pl.Buffered(1))
    row = lambda b, i: (b, i, 0)
    return pl.pallas_call(
        _proj_kernel,
        grid=(bsz, s // tm),
        in_specs=[
            pl.BlockSpec((1, tm, d), row),
            pl.BlockSpec((1, 2, d), lambda b, i: (b, 0, 0)),
            pl.BlockSpec((1, 3, d), lambda b, i: (b, 0, 0)),
            pl.BlockSpec((1, d), lambda b, i: (0, 0)),
            pl.BlockSpec((1, d), lambda b, i: (0, 0)),
            pl.BlockSpec((d, 2 * d), lambda b, i: (0, 0), **const),
            pl.BlockSpec((d, 2 * d), lambda b, i: (0, 0), **const),
        ],
        out_specs=[
            pl.BlockSpec((1, tm, d), row),
            pl.BlockSpec((1, tm, 2 * d), row),
            pl.BlockSpec((1, tm, d), row),
        ],
        out_shape=[
            jax.ShapeDtypeStruct((bsz, s, d), BF16),
            jax.ShapeDtypeStruct((bsz, s, 2 * d), BF16),
            jax.ShapeDtypeStruct((bsz, s, d), F32),
        ],
        compiler_params=pltpu.CompilerParams(
            dimension_semantics=("arbitrary", "arbitrary"),
            vmem_limit_bytes=56 * 1024 * 1024),
        name="projections",
    )(x, modkv, modb, gkv, gb, w_kv, w_qz)


def _suffix_weights():
    j = lax.broadcasted_iota(jnp.int32, (K_BLOCK, K_BLOCK), 0)
    s = lax.broadcasted_iota(jnp.int32, (K_BLOCK, K_BLOCK), 1)
    return jnp.where((j > s) | (s == K_BLOCK - 1), -1.0, 0.0).astype(BF16)


def _attn_kernel(q_ref, k_ref, v_ref, z_ref, w_ref, o_ref,
                 qm_ref, acc_ref, carry_ref, z2a_ref, z2b_ref, aa_ref, ab_ref):
    n_q = q_ref.shape[1] // Q_TILE
    n_groups = q_ref.shape[2] // LANES
    heads = tuple(range(2 * n_groups))
    lane = lax.broadcasted_iota(jnp.int32, (1, LANES), 1)
    head0 = lane < HEAD_DIM
    last_key = lax.broadcasted_iota(jnp.int32, (Q_TILE, K_BLOCK), 1) == K_BLOCK - 1
    quadrant = (Q_TILE // 2, K_BLOCK // 2)
    causal_q = (lax.broadcasted_iota(jnp.int32, quadrant, 1)
                < lax.broadcasted_iota(jnp.int32, quadrant, 0))
    w = w_ref[...]

    def block_start(kb):
        return pl.multiple_of(kb * K_BLOCK, K_BLOCK)

    def logits(h, kb, z2_buf):
        g = h // 2
        k_blk = k_ref[0, pl.ds(block_start(kb), K_BLOCK), g * LANES:(g + 1) * LANES]
        z2_buf[h] = lax.dot_general(qm_ref[h], k_blk, (((1,), (1,)), ((), ())),
                                    preferred_element_type=F32)

    def accumulate(h, a_buf, kb):
        g = h // 2
        v_blk = v_ref[0, pl.ds(block_start(kb), K_BLOCK), g * LANES:(g + 1) * LANES]
        acc_ref[h] += _dot(a_buf[h], v_blk)

    half = Q_TILE // 2

    def softplus2(z2):
        return jnp.maximum(z2, 0.0) + jnp.log(1.0 + jnp.exp2(-jnp.abs(z2))) * LOG2E

    def suffix_sums(h, z2_buf, masked):
        if not masked:
            z2 = z2_buf[h]
            sp = softplus2(z2)
            z2_buf[h] = z2 - sp
            return _dot(sp.astype(BF16), w)
        z_tl = z2_buf[h, :half, :half]
        sp_tl = softplus2(z_tl)
        z2_buf[h, :half, :half] = z_tl - sp_tl
        z_bot = z2_buf[h, half:, :]
        sp_bot = softplus2(z_bot)
        z2_buf[h, half:, :] = z_bot - sp_bot
        sp_top = jnp.concatenate([jnp.where(causal_q, sp_tl, 0.0), jnp.zeros_like(sp_tl)], axis=1)
        sp_bot = jnp.concatenate([sp_bot[:, :half], jnp.where(causal_q, sp_bot[:, half:], 0.0)],
                                 axis=1)
        return _dot(jnp.concatenate([sp_top, sp_bot], axis=0).astype(BF16), w)

    def weights(h, r, z2_buf, a_buf, masked):
        total = jnp.broadcast_to(r[:, K_BLOCK - 1:], (Q_TILE, LANES))
        if not masked:
            carry = carry_ref[h]
            nxt = carry + total
            carry_ref[h] = nxt
            suffix = jnp.where(last_key, 0.0, r)
            ah = jnp.exp2(z2_buf[h] + (suffix + jnp.concatenate([carry] * (K_BLOCK // LANES), axis=1)))
            a_buf[h] = ah.astype(BF16)
            return nxt
        carry_ref[h] = total
        a_tl = jnp.exp2(z2_buf[h, :half, :half] + r[:half, :half])
        a_bl = jnp.exp2(z2_buf[h, half:, :half] + r[half:, :half])
        a_br = jnp.exp2(z2_buf[h, half:, half:] + r[half:, half:])
        a_top = jnp.concatenate([jnp.where(causal_q, a_tl, 0.0), jnp.zeros_like(a_tl)], axis=1)
        a_bot = jnp.concatenate([a_bl, jnp.where(causal_q, a_br, 0.0)], axis=1)
        a_buf[h] = jnp.concatenate([a_top, a_bot], axis=0).astype(BF16)
        return total

    def block_step(kb, kb_prev, variant, masked):
        z2_cur, z2_next = (z2a_ref, z2b_ref) if variant == 0 else (z2b_ref, z2a_ref)
        a_cur, a_prev = (aa_ref, ab_ref) if variant == 0 else (ab_ref, aa_ref)
        if masked:
            for h in heads[:LOGITS_LEAD]:
                logits(h, kb, z2_cur)
        r = {}
        worst = None
        for step in range(len(heads) + WEIGHTS_LAG):
            if masked and step + LOGITS_LEAD < len(heads):
                logits(heads[step + LOGITS_LEAD], kb, z2_cur)
            if step < len(heads):
                h = heads[step]
                r[h] = suffix_sums(h, z2_cur, masked)
                logits(h, jnp.maximum(kb - 1, 0), z2_next)
                accumulate(h, a_prev, kb_prev)
            if step >= WEIGHTS_LAG:
                h = heads[step - WEIGHTS_LAG]
                nxt = weights(h, r.pop(h), z2_cur, a_cur, masked)
                if not masked:
                    worst = nxt if worst is None else jnp.maximum(worst, nxt)
        return None if masked else jnp.max(worst)

    def alternate(variant, fn):
        return lax.cond(variant == 0, lambda: fn(0), lambda: fn(1))

    def finalize(qstart):
        for g in range(n_groups):
            lanes = slice(g * LANES, (g + 1) * LANES)
            zg = z_ref[0, pl.ds(qstart, Q_TILE), lanes]
            acc = jnp.where(head0, acc_ref[2 * g], acc_ref[2 * g + 1])
            o_ref[0, pl.ds(qstart, Q_TILE), lanes] = (acc * _silu(zg)).astype(o_ref.dtype)
        acc_ref[...] = jnp.zeros_like(acc_ref)

    def q_tile(qi, state):
        variant, kb_prev = state
        qstart = pl.multiple_of(qi * Q_TILE, Q_TILE)

        def diagonal_step(v):
            for g in range(n_groups):
                q = q_ref[0, pl.ds(qstart, Q_TILE), g * LANES:(g + 1) * LANES]
                zero = jnp.zeros_like(q)
                qm_ref[2 * g] = jnp.where(head0, q, zero)
                qm_ref[2 * g + 1] = jnp.where(head0, zero, q)
            block_step(qi, kb_prev, v, True)
            finalize(pl.multiple_of(jnp.maximum(qi - 1, 0) * Q_TILE, Q_TILE))

        alternate(variant, diagonal_step)

        def more_blocks(state):
            j, worst, _ = state
            return jnp.logical_and(j < qi, worst > NEGLIGIBLE_LOG2)

        def below_diagonal(state):
            j, _, variant = state
            kb = qi - 1 - j
            worst = alternate(variant, lambda v: block_step(kb, kb + 1, v, False))
            return j + 1, worst, 1 - variant

        n_below, _, variant = lax.while_loop(more_blocks, below_diagonal,
                                             (jnp.int32(0), jnp.float32(0.0), 1 - variant))
        return variant, qi - n_below

    ab_ref[...] = jnp.zeros_like(ab_ref)
    acc_ref[...] = jnp.zeros_like(acc_ref)
    variant, kb_prev = lax.fori_loop(0, n_q, q_tile, (jnp.int32(0), jnp.int32(0)))

    def drain(v):
        a_prev = ab_ref if v == 0 else aa_ref
        for h in heads:
            accumulate(h, a_prev, kb_prev)

    alternate(variant, drain)
    finalize((n_q - 1) * Q_TILE)


def _attention(q, kv, z, *, groups_per_step=4):
    bsz, s, d = q.shape
    width = groups_per_step * LANES
    n_steps = d // width
    n_heads = 2 * groups_per_step
    grp = lambda b, g: (b, 0, g)
    return pl.pallas_call(
        _attn_kernel,
        grid=(bsz, n_steps),
        in_specs=[
            pl.BlockSpec((1, s, width), grp),
            pl.BlockSpec((1, s, width), grp),
            pl.BlockSpec((1, s, width), lambda b, g: (b, 0, n_steps + g)),
            pl.BlockSpec((1, s, width), grp),
            pl.BlockSpec((K_BLOCK, K_BLOCK), lambda b, g: (0, 0)),
        ],
        out_specs=pl.BlockSpec((1, s, width), grp),
        out_shape=jax.ShapeDtypeStruct((bsz, s, d), BF16),
        scratch_shapes=[
            pltpu.VMEM((n_heads, Q_TILE, LANES), BF16),
            pltpu.VMEM((n_heads, Q_TILE, LANES), F32),
            pltpu.VMEM((n_heads, Q_TILE, LANES), F32),
            pltpu.VMEM((n_heads, Q_TILE, K_BLOCK), F32),
            pltpu.VMEM((n_heads, Q_TILE, K_BLOCK), F32),
            pltpu.VMEM((n_heads, Q_TILE, K_BLOCK), BF16),
            pltpu.VMEM((n_heads, Q_TILE, K_BLOCK), BF16),
        ],
        compiler_params=pltpu.CompilerParams(
            dimension_semantics=("arbitrary", "arbitrary"),
            vmem_limit_bytes=40 * 1024 * 1024),
        name="stickbreak_attention",
    )(q, kv, kv, z, _suffix_weights())


def _out_kernel(o_ref, x_ref, mod_ref, wout_ref, g_ref, y_ref):
    gate = mod_ref[0][2:3]
    x = x_ref[0] + gate * _dot(o_ref[0], wout_ref[...].astype(BF16))
    y_ref[0] = (x * lax.rsqrt(jnp.mean(x * x, axis=-1, keepdims=True) + EPS)) * g_ref[...]


def _out_layer(o, x, modb, w_out, g, *, tm=1024):
    bsz, s, d = x.shape
    row = lambda b, i: (b, i, 0)
    return pl.pallas_call(
        _out_kernel,
        grid=(bsz, s // tm),
        in_specs=[
            pl.BlockSpec((1, tm, d), row),
            pl.BlockSpec((1, tm, d), row),
            pl.BlockSpec((1, 3, d), lambda b, i: (b, 0, 0)),
            pl.BlockSpec((d, d), lambda b, i: (0, 0)),
            pl.BlockSpec((1, d), lambda b, i: (0, 0)),
        ],
        out_specs=pl.BlockSpec((1, tm, d), row),
        out_shape=jax.ShapeDtypeStruct((bsz, s, d), F32),
        compiler_params=pltpu.CompilerParams(
            dimension_semantics=("arbitrary", "arbitrary"),
            vmem_limit_bytes=32 * 1024 * 1024),
        name="out_layer",
    )(o, x, modb, w_out, g)


def kernel(x, c, a_mod_w, a_mod_b, a_norm_g, a_w_in, a_conv_w, a_w_out,
           kv_mod_w, kv_mod_b, kv_norm_g, w_kv,
           b_mod_w, b_mod_b, b_norm_g, b_w_qz, b_w_out, final_norm_g):
    bsz, _, d = x.shape
    assert d == D_MODEL and a_mod_w.shape[0] == 1 and b_mod_w.shape[0] == 1

    mod = _modulation(c, [a_mod_w.reshape(d, 3 * d), kv_mod_w, b_mod_w.reshape(d, 3 * d)],
                      [a_mod_b, kv_mod_b, b_mod_b])
    mod_a = mod[:, :3 * d].reshape(bsz, 3, d)
    mod_kv = mod[:, 3 * d:5 * d].reshape(bsz, 2, d)
    mod_b = mod[:, 5 * d:].reshape(bsz, 3, d)

    x1 = _conv_layer(x, mod_a, a_norm_g, a_w_in[0], a_conv_w[0], a_w_out[0])
    q, kv, z = _projections(x1, mod_kv, mod_b, kv_norm_g.reshape(1, d), b_norm_g,
                            w_kv, b_w_qz[0])
    o = _attention(q, kv, z)
    return _out_layer(o, x1, mod_b, b_w_out[0], final_norm_g.reshape(1, d))
```

```python
import functools
import math

import jax
import jax.numpy as jnp
from jax import lax
from jax.experimental import pallas as pl
from jax.experimental.pallas import tpu as pltpu

D_MODEL = 1024
N_HEADS = 16
HEAD_DIM = D_MODEL // N_HEADS
CONV_K = 3
EPS = 1e-6
LOG2E = 1.4426950408889634

LANES = 128
SUBLANES = 8
MXU_DIM = 256

VMEM_BYTES = 64 * 1024 * 1024


def _vmem_limit(fraction):
    return int(VMEM_BYTES * fraction)


ROW_TILE = 1024
MOD_COL_TILE = 1024
HEAD_GROUPS_PER_STEP = 4

K_BLOCK = MXU_DIM
Q_TILE = K_BLOCK
NEGLIGIBLE_LOG2 = -135.0
CONV_ROW_CHUNKS = 4
LOGITS_LEAD = 2

BF16 = jnp.bfloat16
F32 = jnp.float32


def _dot(a, b):
    return jnp.dot(a, b, preferred_element_type=F32)


def _silu(x):
    return x * jax.nn.sigmoid(x)


def _mod_kernel(c_ref, *refs, starts):
    n = len(starts) - 1
    w_refs, b_refs, o_ref = refs[:n], refs[n:2 * n], refs[2 * n]
    j = pl.program_id(0)
    s = _silu(c_ref[...]).astype(BF16)
    for i in range(n):
        @pl.when(jnp.logical_and(j >= starts[i], j < starts[i + 1]))
        def _(i=i):
            o_ref[...] = _dot(s, w_refs[i][...].astype(BF16)) + b_refs[i][...]


def _modulation(c, ws, bs):
    bsz, d = c.shape
    tn = MOD_COL_TILE
    tiles = [w.shape[1] // tn for w in ws]
    starts = [sum(tiles[:i]) for i in range(len(ws) + 1)]

    def tile_of(i):
        return lambda j: (0, jnp.clip(j - starts[i], 0, tiles[i] - 1))

    return pl.pallas_call(
        functools.partial(_mod_kernel, starts=tuple(starts)),
        grid=(starts[-1],),
        in_specs=([pl.BlockSpec((bsz, d), lambda j: (0, 0))]
                  + [pl.BlockSpec((d, tn), tile_of(i)) for i in range(len(ws))]
                  + [pl.BlockSpec((1, tn), tile_of(i)) for i in range(len(ws))]),
        out_specs=pl.BlockSpec((bsz, tn), lambda j: (0, j)),
        out_shape=jax.ShapeDtypeStruct((bsz, starts[-1] * tn), F32),
        compiler_params=pltpu.CompilerParams(
            dimension_semantics=("arbitrary",),
            vmem_limit_bytes=_vmem_limit(5 / 8)),
        name="modulation",
    )(c, *ws, *[b.reshape(1, -1) for b in bs])


def _conv_layer_kernel(x_ref, mod_ref, g_ref, win_ref, cw_ref, wout_ref, o_ref, carry_ref):
    tm = x_ref.shape[1]
    d = x_ref.shape[2]

    @pl.when(pl.program_id(1) == 0)
    def _():
        carry_ref[...] = jnp.zeros_like(carry_ref)

    mod = mod_ref[0]
    shift, scale, gate = mod[0:1], mod[1:2], mod[2:3]
    gain = g_ref[...] * (1.0 + scale)
    cw = cw_ref[...]
    w_in = [win_ref[:, p * d:(p + 1) * d].astype(BF16) for p in range(4)]
    w_out = wout_ref[...].astype(BF16)

    rows = tm // CONV_ROW_CHUNKS
    tail = carry_ref[...]
    for r0 in range(0, tm, rows):
        x = x_ref[0, r0:r0 + rows, :]
        rs = lax.rsqrt(jnp.mean(x * x, axis=-1, keepdims=True) + EPS)
        h = ((x * rs) * gain + shift).astype(BF16)
        b_gate, c_gate, u, z = (_dot(h, w) for w in w_in)

        cu = c_gate * u
        ext = jnp.concatenate([tail, cu], axis=0)
        prev1 = pltpu.roll(ext, 1, 0)[SUBLANES:]
        prev2 = pltpu.roll(ext, 2, 0)[SUBLANES:]
        tail = cu[rows - SUBLANES:]
        conv = cw[0:1] * prev2 + cw[1:2] * prev1 + cw[2:3] * cu

        y = (b_gate * conv) * _silu(z)
        o_ref[0, r0:r0 + rows, :] = x + gate * _dot(y.astype(BF16), w_out)
    carry_ref[...] = tail


def _conv_layer(x, mod, g, w_in, conv_w, w_out):
    bsz, s, d = x.shape
    tm = ROW_TILE
    const = dict(pipeline_mode=pl.Buffered(1))
    return pl.pallas_call(
        _conv_layer_kernel,
        grid=(bsz, s // tm),
        in_specs=[
            pl.BlockSpec((1, tm, d), lambda b, i: (b, i, 0)),
            pl.BlockSpec((1, 3, d), lambda b, i: (b, 0, 0)),
            pl.BlockSpec((1, d), lambda b, i: (0, 0)),
            pl.BlockSpec((d, 4 * d), lambda b, i: (0, 0), **const),
            pl.BlockSpec((CONV_K, d), lambda b, i: (0, 0)),
            pl.BlockSpec((d, d), lambda b, i: (0, 0), **const),
        ],
        out_specs=pl.BlockSpec((1, tm, d), lambda b, i: (b, i, 0)),
        out_shape=jax.ShapeDtypeStruct((bsz, s, d), F32),
        scratch_shapes=[pltpu.VMEM((SUBLANES, d), F32)],
        compiler_params=pltpu.CompilerParams(
            dimension_semantics=("arbitrary", "arbitrary"),
            vmem_limit_bytes=_vmem_limit(7 / 8)),
        name="conv_layer",
    )(x, mod, g, w_in, conv_w, w_out)


def _proj_kernel(x_ref, modkv_ref, modb_ref, gkv_ref, gb_ref, wkv_ref, wqz_ref,
                 q_ref, kv_ref, z_ref):
    d = x_ref.shape[2]
    x = x_ref[0]
    xn = x * lax.rsqrt(jnp.mean(x * x, axis=-1, keepdims=True) + EPS)
    modkv = modkv_ref[0]
    modb = modb_ref[0]
    hkv = (xn * (gkv_ref[...] * (1.0 + modkv[1:2])) + modkv[0:1]).astype(BF16)
    hq = (xn * (gb_ref[...] * (1.0 + modb[1:2])) + modb[0:1]).astype(BF16)
    kv_ref[0] = _dot(hkv, wkv_ref[...].astype(BF16)).astype(BF16)
    q_ref[0] = (_dot(hq, wqz_ref[:, :d].astype(BF16)) * (LOG2E / math.sqrt(HEAD_DIM))).astype(BF16)
    z_ref[0] = _dot(hq, wqz_ref[:, d:].astype(BF16))


def _projections(x, modkv, modb, gkv, gb, w_kv, w_qz):
    bsz, s, d = x.shape
    tm = ROW_TILE
    const = dict(pipeline_mode=pl.Buffered(1))
    row = lambda b, i: (b, i, 0)
    return pl.pallas_call(
        _proj_kernel,
        grid=(bsz, s // tm),
        in_specs=[
            pl.BlockSpec((1, tm, d), row),
            pl.BlockSpec((1, 2, d), lambda b, i: (b, 0, 0)),
            pl.BlockSpec((1, 3, d), lambda b, i: (b, 0, 0)),
            pl.BlockSpec((1, d), lambda b, i: (0, 0)),
            pl.BlockSpec((1, d), lambda b, i: (0, 0)),
            pl.BlockSpec((d, 2 * d), lambda b, i: (0, 0), **const),
            pl.BlockSpec((d, 2 * d), lambda b, i: (0, 0), **const),
        ],
        out_specs=[
            pl.BlockSpec((1, tm, d), row),
            pl.BlockSpec((1, tm, 2 * d), row),
            pl.BlockSpec((1, tm, d), row),
        ],
        out_shape=[
            jax.ShapeDtypeStruct((bsz, s, d), BF16),
            jax.ShapeDtypeStruct((bsz, s, 2 * d), BF16),
            jax.ShapeDtypeStruct((bsz, s, d), F32),
        ],
        compiler_params=pltpu.CompilerParams(
            dimension_semantics=("arbitrary", "arbitrary"),
            vmem_limit_bytes=_vmem_limit(7 / 8)),
        name="projections",
    )(x, modkv, modb, gkv, gb, w_kv, w_qz)


def _suffix_weights():
    j = lax.broadcasted_iota(jnp.int32, (K_BLOCK, K_BLOCK), 0)
    s = lax.broadcasted_iota(jnp.int32, (K_BLOCK, K_BLOCK), 1)
    return jnp.where((j > s) | (s == K_BLOCK - 1), -1.0, 0.0).astype(BF16)


def _attn_kernel(q_ref, k_ref, v_ref, z_ref, w_ref, o_ref,
                 qm_ref, acc_ref, carry_ref, z2a_ref, z2b_ref, aa_ref, ab_ref):
    n_q = q_ref.shape[1] // Q_TILE
    n_groups = q_ref.shape[2] // LANES
    heads = tuple(range(2 * n_groups))
    lane = lax.broadcasted_iota(jnp.int32, (1, LANES), 1)
    head0 = lane < HEAD_DIM
    last_key = lax.broadcasted_iota(jnp.int32, (Q_TILE, K_BLOCK), 1) == K_BLOCK - 1
    quadrant = (Q_TILE // 2, K_BLOCK // 2)
    causal_q = (lax.broadcasted_iota(jnp.int32, quadrant, 1)
                < lax.broadcasted_iota(jnp.int32, quadrant, 0))
    w = w_ref[...]

    def block_start(kb):
        return pl.multiple_of(kb * K_BLOCK, K_BLOCK)

    def logits(h, kb, z2_buf):
        g = h // 2
        k_blk = k_ref[0, pl.ds(block_start(kb), K_BLOCK), g * LANES:(g + 1) * LANES]
        z2_buf[h] = lax.dot_general(qm_ref[h], k_blk, (((1,), (1,)), ((), ())),
                                    preferred_element_type=F32)

    def accumulate(h, a_buf, kb):
        g = h // 2
        v_blk = v_ref[0, pl.ds(block_start(kb), K_BLOCK), g * LANES:(g + 1) * LANES]
        acc_ref[h] += _dot(a_buf[h], v_blk)

    half = Q_TILE // 2

    def softplus2(z2):
        return jnp.maximum(z2, 0.0) + jnp.log(1.0 + jnp.exp2(-jnp.abs(z2))) * LOG2E

    def suffix_sums(h, z2_buf, masked):
        if not masked:
            z2 = z2_buf[h]
            sp = softplus2(z2)
            z2_buf[h] = z2 - sp
            return _dot(sp.astype(BF16), w)
        z_tl = z2_buf[h, :half, :half]
        sp_tl = softplus2(z_tl)
        z2_buf[h, :half, :half] = z_tl - sp_tl
        z_bot = z2_buf[h, half:, :]
        sp_bot = softplus2(z_bot)
        z2_buf[h, half:, :] = z_bot - sp_bot
        sp_top = jnp.concatenate([jnp.where(causal_q, sp_tl, 0.0), jnp.zeros_like(sp_tl)], axis=1)
        sp_bot = jnp.concatenate([sp_bot[:, :half], jnp.where(causal_q, sp_bot[:, half:], 0.0)],
                                 axis=1)
        return _dot(jnp.concatenate([sp_top, sp_bot], axis=0).astype(BF16), w)

    def weights(h, r, z2_buf, a_buf, masked):
        total = jnp.broadcast_to(r[:, K_BLOCK - 1:], (Q_TILE, LANES))
        if not masked:
            carry = carry_ref[h]
            nxt = carry + total
            carry_ref[h] = nxt
            suffix = jnp.where(last_key, 0.0, r)
            ah = jnp.exp2(z2_buf[h] + (suffix + jnp.concatenate([carry] * (K_BLOCK // LANES), axis=1)))
            a_buf[h] = ah.astype(BF16)
            return nxt
        carry_ref[h] = total
        a_tl = jnp.exp2(z2_buf[h, :half, :half] + r[:half, :half])
        a_bl = jnp.exp2(z2_buf[h, half:, :half] + r[half:, :half])
        a_br = jnp.exp2(z2_buf[h, half:, half:] + r[half:, half:])
        a_top = jnp.concatenate([jnp.where(causal_q, a_tl, 0.0), jnp.zeros_like(a_tl)], axis=1)
        a_bot = jnp.concatenate([a_bl, jnp.where(causal_q, a_br, 0.0)], axis=1)
        a_buf[h] = jnp.concatenate([a_top, a_bot], axis=0).astype(BF16)
        return total

    def block_step(kb, kb_prev, variant, masked):
        z2_cur, z2_next = (z2a_ref, z2b_ref) if variant == 0 else (z2b_ref, z2a_ref)
        a_cur, a_prev = (aa_ref, ab_ref) if variant == 0 else (ab_ref, aa_ref)
        if masked:
            for h in heads[:LOGITS_LEAD]:
                logits(h, kb, z2_cur)
        worst = None
        for i, h in enumerate(heads):
            if masked and i + LOGITS_LEAD < len(heads):
                logits(heads[i + LOGITS_LEAD], kb, z2_cur)
            r = suffix_sums(h, z2_cur, masked)
            logits(h, jnp.maximum(kb - 1, 0), z2_next)
            accumulate(h, a_prev, kb_prev)
            nxt = weights(h, r, z2_cur, a_cur, masked)
            if not masked:
                worst = nxt if worst is None else jnp.maximum(worst, nxt)
        return None if masked else jnp.max(worst)

    def alternate(variant, fn):
        return lax.cond(variant == 0, lambda: fn(0), lambda: fn(1))

    def finalize(qstart):
        for g in range(n_groups):
            lanes = slice(g * LANES, (g + 1) * LANES)
            zg = z_ref[0, pl.ds(qstart, Q_TILE), lanes]
            acc = jnp.where(head0, acc_ref[2 * g], acc_ref[2 * g + 1])
            o_ref[0, pl.ds(qstart, Q_TILE), lanes] = (acc * _silu(zg)).astype(o_ref.dtype)
        acc_ref[...] = jnp.zeros_like(acc_ref)

    def q_tile(qi, state):
        variant, kb_prev = state
        qstart = pl.multiple_of(qi * Q_TILE, Q_TILE)

        def diagonal_step(v):
            for g in range(n_groups):
                q = q_ref[0, pl.ds(qstart, Q_TILE), g * LANES:(g + 1) * LANES]
                zero = jnp.zeros_like(q)
                qm_ref[2 * g] = jnp.where(head0, q, zero)
                qm_ref[2 * g + 1] = jnp.where(head0, zero, q)
            block_step(qi, kb_prev, v, True)
            finalize(pl.multiple_of(jnp.maximum(qi - 1, 0) * Q_TILE, Q_TILE))

        alternate(variant, diagonal_step)

        def more_blocks(state):
            j, worst, _ = state
            return jnp.logical_and(j < qi, worst > NEGLIGIBLE_LOG2)

        def below_diagonal(state):
            j, _, variant = state
            kb = qi - 1 - j
            worst = alternate(variant, lambda v: block_step(kb, kb + 1, v, False))
            return j + 1, worst, 1 - variant

        n_below, _, variant = lax.while_loop(more_blocks, below_diagonal,
                                             (jnp.int32(0), jnp.float32(0.0), 1 - variant))
        return variant, qi - n_below

    ab_ref[...] = jnp.zeros_like(ab_ref)
    acc_ref[...] = jnp.zeros_like(acc_ref)
    variant, kb_prev = lax.fori_loop(0, n_q, q_tile, (jnp.int32(0), jnp.int32(0)))

    def drain(v):
        a_prev = ab_ref if v == 0 else aa_ref
        for h in heads:
            accumulate(h, a_prev, kb_prev)

    alternate(variant, drain)
    finalize((n_q - 1) * Q_TILE)


def _attention(q, kv, z):
    bsz, s, d = q.shape
    width = HEAD_GROUPS_PER_STEP * LANES
    n_steps = d // width
    n_heads = 2 * HEAD_GROUPS_PER_STEP
    grp = lambda b, g: (b, 0, g)
    return pl.pallas_call(
        _attn_kernel,
        grid=(bsz, n_steps),
        in_specs=[
            pl.BlockSpec((1, s, width), grp),
            pl.BlockSpec((1, s, width), grp),
            pl.BlockSpec((1, s, width), lambda b, g: (b, 0, n_steps + g)),
            pl.BlockSpec((1, s, width), grp),
            pl.BlockSpec((K_BLOCK, K_BLOCK), lambda b, g: (0, 0)),
        ],
        out_specs=pl.BlockSpec((1, s, width), grp),
        out_shape=jax.ShapeDtypeStruct((bsz, s, d), BF16),
        scratch_shapes=[
            pltpu.VMEM((n_heads, Q_TILE, LANES), BF16),
            pltpu.VMEM((n_heads, Q_TILE, LANES), F32),
            pltpu.VMEM((n_heads, Q_TILE, LANES), F32),
            pltpu.VMEM((n_heads, Q_TILE, K_BLOCK), F32),
            pltpu.VMEM((n_heads, Q_TILE, K_BLOCK), F32),
            pltpu.VMEM((n_heads, Q_TILE, K_BLOCK), BF16),
            pltpu.VMEM((n_heads, Q_TILE, K_BLOCK), BF16),
        ],
        compiler_params=pltpu.CompilerParams(
            dimension_semantics=("arbitrary", "arbitrary"),
            vmem_limit_bytes=_vmem_limit(5 / 8)),
        name="stickbreak_attention",
    )(q, kv, kv, z, _suffix_weights())


def _out_kernel(o_ref, x_ref, mod_ref, wout_ref, g_ref, y_ref):
    gate = mod_ref[0][2:3]
    x = x_ref[0] + gate * _dot(o_ref[0], wout_ref[...].astype(BF16))
    y_ref[0] = (x * lax.rsqrt(jnp.mean(x * x, axis=-1, keepdims=True) + EPS)) * g_ref[...]


def _out_layer(o, x, modb, w_out, g):
    bsz, s, d = x.shape
    tm = ROW_TILE
    row = lambda b, i: (b, i, 0)
    return pl.pallas_call(
        _out_kernel,
        grid=(bsz, s // tm),
        in_specs=[
            pl.BlockSpec((1, tm, d), row),
            pl.BlockSpec((1, tm, d), row),
            pl.BlockSpec((1, 3, d), lambda b, i: (b, 0, 0)),
            pl.BlockSpec((d, d), lambda b, i: (0, 0)),
            pl.BlockSpec((1, d), lambda b, i: (0, 0)),
        ],
        out_specs=pl.BlockSpec((1, tm, d), row),
        out_shape=jax.ShapeDtypeStruct((bsz, s, d), F32),
        compiler_params=pltpu.CompilerParams(
            dimension_semantics=("arbitrary", "arbitrary"),
            vmem_limit_bytes=_vmem_limit(1 / 2)),
        name="out_layer",
    )(o, x, modb, w_out, g)


def kernel(x, c, a_mod_w, a_mod_b, a_norm_g, a_w_in, a_conv_w, a_w_out,
           kv_mod_w, kv_mod_b, kv_norm_g, w_kv,
           b_mod_w, b_mod_b, b_norm_g, b_w_qz, b_w_out, final_norm_g):
    bsz, _, d = x.shape
    assert d == D_MODEL and a_mod_w.shape[0] == 1 and b_mod_w.shape[0] == 1

    mod = _modulation(c, [a_mod_w.reshape(d, 3 * d), kv_mod_w, b_mod_w.reshape(d, 3 * d)],
                      [a_mod_b, kv_mod_b, b_mod_b])
    mod_a = mod[:, :3 * d].reshape(bsz, 3, d)
    mod_kv = mod[:, 3 * d:5 * d].reshape(bsz, 2, d)
    mod_b = mod[:, 5 * d:].reshape(bsz, 3, d)

    x1 = _conv_layer(x, mod_a, a_norm_g, a_w_in[0], a_conv_w[0], a_w_out[0])
    q, kv, z = _projections(x1, mod_kv, mod_b, kv_norm_g.reshape(1, d), b_norm_g,
                            w_kv, b_w_qz[0])
    o = _attention(q, kv, z)
    return _out_layer(o, x1, mod_b, b_w_out[0], final_norm_g.reshape(1, d))
```

```python
import functools
import math

import jax
import jax.numpy as jnp
from jax import lax
from jax.experimental import pallas as pl
from jax.experimental.pallas import tpu as pltpu

D_MODEL = 1024
N_HEADS = 16
HEAD_DIM = D_MODEL // N_HEADS
CONV_K = 3
EPS = 1e-6
LOG2E = 1.4426950408889634

LANES = 128
SUBLANES = 8
MXU_DIM = 256

VMEM_BYTES = 64 * 1024 * 1024


def _vmem_limit(fraction):
    return int(VMEM_BYTES * fraction)


ROW_TILE = 1024
MOD_COL_TILE = 1024
HEAD_GROUPS_PER_STEP = 4

K_BLOCK = MXU_DIM
Q_TILE = K_BLOCK
NEGLIGIBLE_LOG2 = -135.0
CONV_ROW_CHUNKS = 4
LOGITS_LEAD = 2

BF16 = jnp.bfloat16
F32 = jnp.float32


def _dot(a, b):
    return jnp.dot(a, b, preferred_element_type=F32)


def _silu(x):
    return x * jax.nn.sigmoid(x)


def _mod_kernel(c_ref, *refs, starts):
    n = len(starts) - 1
    w_refs, b_refs, o_ref = refs[:n], refs[n:2 * n], refs[2 * n]
    j = pl.program_id(0)
    s = _silu(c_ref[...]).astype(BF16)
    for i in range(n):
        @pl.when(jnp.logical_and(j >= starts[i], j < starts[i + 1]))
        def _(i=i):
            o_ref[...] = _dot(s, w_refs[i][...].astype(BF16)) + b_refs[i][...]


def _modulation(c, ws, bs):
    bsz, d = c.shape
    tn = MOD_COL_TILE
    tiles = [w.shape[1] // tn for w in ws]
    starts = [sum(tiles[:i]) for i in range(len(ws) + 1)]

    def tile_of(i):
        return lambda j: (0, jnp.clip(j - starts[i], 0, tiles[i] - 1))

    return pl.pallas_call(
        functools.partial(_mod_kernel, starts=tuple(starts)),
        grid=(starts[-1],),
        in_specs=([pl.BlockSpec((bsz, d), lambda j: (0, 0))]
                  + [pl.BlockSpec((d, tn), tile_of(i)) for i in range(len(ws))]
                  + [pl.BlockSpec((1, tn), tile_of(i)) for i in range(len(ws))]),
        out_specs=pl.BlockSpec((bsz, tn), lambda j: (0, j)),
        out_shape=jax.ShapeDtypeStruct((bsz, starts[-1] * tn), F32),
        compiler_params=pltpu.CompilerParams(
            dimension_semantics=("arbitrary",),
            vmem_limit_bytes=_vmem_limit(5 / 8)),
        name="modulation",
    )(c, *ws, *[b.reshape(1, -1) for b in bs])


def _conv_layer_kernel(x_ref, mod_ref, g_ref, win_ref, cw_ref, wout_ref, o_ref, carry_ref):
    tm = x_ref.shape[1]
    d = x_ref.shape[2]

    @pl.when(pl.program_id(1) == 0)
    def _():
        carry_ref[...] = jnp.zeros_like(carry_ref)

    mod = mod_ref[0]
    shift, scale, gate = mod[0:1], mod[1:2], mod[2:3]
    gain = g_ref[...] * (1.0 + scale)
    cw = cw_ref[...]
    w_in = [win_ref[:, p * d:(p + 1) * d].astype(BF16) for p in range(4)]
    w_out = wout_ref[...].astype(BF16)

    rows = tm // CONV_ROW_CHUNKS
    tail = carry_ref[...]
    for r0 in range(0, tm, rows):
        x = x_ref[0, r0:r0 + rows, :]
        rs = lax.rsqrt(jnp.mean(x * x, axis=-1, keepdims=True) + EPS)
        h = ((x * rs) * gain + shift).astype(BF16)
        b_gate, c_gate, u, z = (_dot(h, w) for w in w_in)

        cu = c_gate * u
        ext = jnp.concatenate([tail, cu], axis=0)
        prev1 = pltpu.roll(ext, 1, 0)[SUBLANES:]
        prev2 = pltpu.roll(ext, 2, 0)[SUBLANES:]
        tail = cu[rows - SUBLANES:]
        conv = cw[0:1] * prev2 + cw[1:2] * prev1 + cw[2:3] * cu

        y = (b_gate * conv) * _silu(z)
        o_ref[0, r0:r0 + rows, :] = x + gate * _dot(y.astype(BF16), w_out)
    carry_ref[...] = tail


def _conv_layer(x, mod, g, w_in, conv_w, w_out):
    bsz, s, d = x.shape
    tm = ROW_TILE
    const = dict(pipeline_mode=pl.Buffered(1))
    return pl.pallas_call(
        _conv_layer_kernel,
        grid=(bsz, s // tm),
        in_specs=[
            pl.BlockSpec((1, tm, d), lambda b, i: (b, i, 0)),
            pl.BlockSpec((1, 3, d), lambda b, i: (b, 0, 0)),
            pl.BlockSpec((1, d), lambda b, i: (0, 0)),
            pl.BlockSpec((d, 4 * d), lambda b, i: (0, 0), **const),
            pl.BlockSpec((CONV_K, d), lambda b, i: (0, 0)),
            pl.BlockSpec((d, d), lambda b, i: (0, 0), **const),
        ],
        out_specs=pl.BlockSpec((1, tm, d), lambda b, i: (b, i, 0)),
        out_shape=jax.ShapeDtypeStruct((bsz, s, d), F32),
        scratch_shapes=[pltpu.VMEM((SUBLANES, d), F32)],
        compiler_params=pltpu.CompilerParams(
            dimension_semantics=("arbitrary", "arbitrary"),
            vmem_limit_bytes=_vmem_limit(7 / 8)),
        name="conv_layer",
    )(x, mod, g, w_in, conv_w, w_out)


def _proj_kernel(x_ref, modkv_ref, modb_ref, gkv_ref, gb_ref, wkv_ref, wqz_ref,
                 q_ref, kv_ref, z_ref):
    d = x_ref.shape[2]
    x = x_ref[0]
    xn = x * lax.rsqrt(jnp.mean(x * x, axis=-1, keepdims=True) + EPS)
    modkv = modkv_ref[0]
    modb = modb_ref[0]
    hkv = (xn * (gkv_ref[...] * (1.0 + modkv[1:2])) + modkv[0:1]).astype(BF16)
    hq = (xn * (gb_ref[...] * (1.0 + modb[1:2])) + modb[0:1]).astype(BF16)
    kv_ref[0] = _dot(hkv, wkv_ref[...].astype(BF16)).astype(BF16)
    q_ref[0] = (_dot(hq, wqz_ref[:, :d].astype(BF16)) * (LOG2E / math.sqrt(HEAD_DIM))).astype(BF16)
    z_ref[0] = _dot(hq, wqz_ref[:, d:].astype(BF16))


def _projections(x, modkv, modb, gkv, gb, w_kv, w_qz):
    bsz, s, d = x.shape
    tm = ROW_TILE
    const = dict(pipeline_mode=pl.Buffered(1))
    row = lambda b, i: (b, i, 0)
    return pl.pallas_call(
        _proj_kernel,
        grid=(bsz, s // tm),
        in_specs=[
            pl.BlockSpec((1, tm, d), row),
            pl.BlockSpec((1, 2, d), lambda b, i: (b, 0, 0)),
            pl.BlockSpec((1, 3, d), lambda b, i: (b, 0, 0)),
            pl.BlockSpec((1, d), lambda b, i: (0, 0)),
            pl.BlockSpec((1, d), lambda b, i: (0, 0)),
            pl.BlockSpec((d, 2 * d), lambda b, i: (0, 0), **const),
            pl.BlockSpec((d, 2 * d), lambda b, i: (0, 0), **const),
        ],
        out_specs=[
            pl.BlockSpec((1, tm, d), row),
            pl.BlockSpec((1, tm, 2 * d), row),
            pl.BlockSpec((1, tm, d), row),
        ],
        out_shape=[
            jax.ShapeDtypeStruct((bsz, s, d), BF16),
            jax.ShapeDtypeStruct((bsz, s, 2 * d), BF16),
            jax.ShapeDtypeStruct((bsz, s, d), F32),
        ],
        compiler_params=pltpu.CompilerParams(
            dimension_semantics=("arbitrary", "arbitrary"),
            vmem_limit_bytes=_vmem_limit(7 / 8)),
        name="projections",
    )(x, modkv, modb, gkv, gb, w_kv, w_qz)


def _suffix_weights():
    j = lax.broadcasted_iota(jnp.int32, (K_BLOCK, K_BLOCK), 0)
    s = lax.broadcasted_iota(jnp.int32, (K_BLOCK, K_BLOCK), 1)
    return jnp.where((j > s) | (s == K_BLOCK - 1), -1.0, 0.0).astype(BF16)


def _attn_kernel(q_ref, k_ref, v_ref, z_ref, w_ref, o_ref,
                 qm_ref, acc_ref, carry_ref, z2a_ref, z2b_ref, aa_ref, ab_ref):
    n_q = q_ref.shape[1] // Q_TILE
    n_groups = q_ref.shape[2] // LANES
    heads = tuple(range(2 * n_groups))
    lane = lax.broadcasted_iota(jnp.int32, (1, LANES), 1)
    head0 = lane < HEAD_DIM
    last_key = lax.broadcasted_iota(jnp.int32, (Q_TILE, K_BLOCK), 1) == K_BLOCK - 1
    quadrant = (Q_TILE // 2, K_BLOCK // 2)
    causal_q = (lax.broadcasted_iota(jnp.int32, quadrant, 1)
                < lax.broadcasted_iota(jnp.int32, quadrant, 0))
    w = w_ref[...]

    def block_start(kb):
        return pl.multiple_of(kb * K_BLOCK, K_BLOCK)

    def logits(h, kb, z2_buf):
        g = h // 2
        k_blk = k_ref[0, pl.ds(block_start(kb), K_BLOCK), g * LANES:(g + 1) * LANES]
        z2_buf[h] = lax.dot_general(qm_ref[h], k_blk, (((1,), (1,)), ((), ())),
                                    preferred_element_type=F32)

    def accumulate(h, a_buf, kb):
        g = h // 2
        v_blk = v_ref[0, pl.ds(block_start(kb), K_BLOCK), g * LANES:(g + 1) * LANES]
        acc_ref[h] += _dot(a_buf[h], v_blk)

    half = Q_TILE // 2

    def softplus2(z2):
        return jnp.maximum(z2, 0.0) + jnp.log(1.0 + jnp.exp2(-jnp.abs(z2))) * LOG2E

    def suffix_sums(h, z2_buf, masked):
        if not masked:
            z2 = z2_buf[h]
            sp = softplus2(z2)
            z2_buf[h] = z2 - sp
            return _dot(sp.astype(BF16), w)
        z_tl = z2_buf[h, :half, :half]
        sp_tl = softplus2(z_tl)
        z2_buf[h, :half, :half] = z_tl - sp_tl
        z_bot = z2_buf[h, half:, :]
        sp_bot = softplus2(z_bot)
        z2_buf[h, half:, :] = z_bot - sp_bot
        sp_top = jnp.concatenate([jnp.where(causal_q, sp_tl, 0.0), jnp.zeros_like(sp_tl)], axis=1)
        sp_bot = jnp.concatenate([sp_bot[:, :half], jnp.where(causal_q, sp_bot[:, half:], 0.0)],
                                 axis=1)
        return _dot(jnp.concatenate([sp_top, sp_bot], axis=0).astype(BF16), w)

    def weights(h, r, z2_buf, a_buf, masked):
        total = jnp.broadcast_to(r[:, K_BLOCK - 1:], (Q_TILE, LANES))
        if not masked:
            carry = carry_ref[h]
            nxt = carry + total
            carry_ref[h] = nxt
            suffix = jnp.where(last_key, 0.0, r)
            ah = jnp.exp2(z2_buf[h] + (suffix + jnp.concatenate([carry] * (K_BLOCK // LANES), axis=1)))
            a_buf[h] = ah.astype(BF16)
            return nxt
        carry_ref[h] = total
        a_tl = jnp.exp2(z2_buf[h, :half, :half] + r[:half, :half])
        a_bl = jnp.exp2(z2_buf[h, half:, :half] + r[half:, :half])
        a_br = jnp.exp2(z2_buf[h, half:, half:] + r[half:, half:])
        a_top = jnp.concatenate([jnp.where(causal_q, a_tl, 0.0), jnp.zeros_like(a_tl)], axis=1)
        a_bot = jnp.concatenate([a_bl, jnp.where(causal_q, a_br, 0.0)], axis=1)
        a_buf[h] = jnp.concatenate([a_top, a_bot], axis=0).astype(BF16)
        return total

    def block_step(kb, kb_prev, variant, masked):
        z2_cur, z2_next = (z2a_ref, z2b_ref) if variant == 0 else (z2b_ref, z2a_ref)
        a_cur, a_prev = (aa_ref, ab_ref) if variant == 0 else (ab_ref, aa_ref)
        if masked:
            for h in heads[:LOGITS_LEAD]:
                logits(h, kb, z2_cur)
        worst = None
        for i, h in enumerate(heads):
            if masked and i + LOGITS_LEAD < len(heads):
                logits(heads[i + LOGITS_LEAD], kb, z2_cur)
            r = suffix_sums(h, z2_cur, masked)
            logits(h, jnp.maximum(kb - 1, 0), z2_next)
            accumulate(h, a_prev, kb_prev)
            nxt = weights(h, r, z2_cur, a_cur, masked)
            if not masked:
                worst = nxt if worst is None else jnp.maximum(worst, nxt)
        return None if masked else jnp.max(worst)

    def alternate(variant, fn):
        return lax.cond(variant == 0, lambda: fn(0), lambda: fn(1))

    def finalize(qstart):
        for g in range(n_groups):
            lanes = slice(g * LANES, (g + 1) * LANES)
            zg = z_ref[0, pl.ds(qstart, Q_TILE), lanes]
            acc = jnp.where(head0, acc_ref[2 * g], acc_ref[2 * g + 1])
            o_ref[0, pl.ds(qstart, Q_TILE), lanes] = (acc * _silu(zg)).astype(o_ref.dtype)
        acc_ref[...] = jnp.zeros_like(acc_ref)

    def q_tile(qi, state):
        variant, kb_prev = state
        qstart = pl.multiple_of(qi * Q_TILE, Q_TILE)

        def diagonal_step(v):
            for g in range(n_groups):
                q = q_ref[0, pl.ds(qstart, Q_TILE), g * LANES:(g + 1) * LANES]
                zero = jnp.zeros_like(q)
                qm_ref[2 * g] = jnp.where(head0, q, zero)
                qm_ref[2 * g + 1] = jnp.where(head0, zero, q)
            block_step(qi, kb_prev, v, True)
            finalize(pl.multiple_of(jnp.maximum(qi - 1, 0) * Q_TILE, Q_TILE))

        alternate(variant, diagonal_step)

        def more_blocks(state):
            j, worst, _ = state
            return jnp.logical_and(j < qi, worst > NEGLIGIBLE_LOG2)

        def below_diagonal(state):
            j, _, variant = state
            kb = qi - 1 - j
            worst = alternate(variant, lambda v: block_step(kb, kb + 1, v, False))
            return j + 1, worst, 1 - variant

        n_below, _, variant = lax.while_loop(more_blocks, below_diagonal,
                                             (jnp.int32(0), jnp.float32(0.0), 1 - variant))
        return variant, qi - n_below

    ab_ref[...] = jnp.zeros_like(ab_ref)
    acc_ref[...] = jnp.zeros_like(acc_ref)
    variant, kb_prev = lax.fori_loop(0, n_q, q_tile, (jnp.int32(0), jnp.int32(0)))

    def drain(v):
        a_prev = ab_ref if v == 0 else aa_ref
        for h in heads:
            accumulate(h, a_prev, kb_prev)

    alternate(variant, drain)
    finalize((n_q - 1) * Q_TILE)


def _attention(q, kv, z):
    bsz, s, d = q.shape
    width = HEAD_GROUPS_PER_STEP * LANES
    n_steps = d // width
    n_heads = 2 * HEAD_GROUPS_PER_STEP
    grp = lambda b, g: (b, 0, g)
    return pl.pallas_call(
        _attn_kernel,
        grid=(bsz, n_steps),
        in_specs=[
            pl.BlockSpec((1, s, width), grp),
            pl.BlockSpec((1, s, width), grp),
            pl.BlockSpec((1, s, width), lambda b, g: (b, 0, n_steps + g)),
            pl.BlockSpec((1, s, width), grp),
            pl.BlockSpec((K_BLOCK, K_BLOCK), lambda b, g: (0, 0)),
        ],
        out_specs=pl.BlockSpec((1, s, width), grp),
        out_shape=jax.ShapeDtypeStruct((bsz, s, d), BF16),
        scratch_shapes=[
            pltpu.VMEM((n_heads, Q_TILE, LANES), BF16),
            pltpu.VMEM((n_heads, Q_TILE, LANES), F32),
            pltpu.VMEM((n_heads, Q_TILE, LANES), F32),
            pltpu.VMEM((n_heads, Q_TILE, K_BLOCK), F32),
            pltpu.VMEM((n_heads, Q_TILE, K_BLOCK), F32),
            pltpu.VMEM((n_heads, Q_TILE, K_BLOCK), BF16),
            pltpu.VMEM((n_heads, Q_TILE, K_BLOCK), BF16),
        ],
        compiler_params=pltpu.CompilerParams(
            dimension_semantics=("arbitrary", "arbitrary"),
            vmem_limit_bytes=_vmem_limit(5 / 8)),
        name="stickbreak_attention",
    )(q, kv, kv, z, _suffix_weights())


def _out_kernel(o_ref, x_ref, mod_ref, wout_ref, g_ref, y_ref):
    gate = mod_ref[0][2:3]
    x = x_ref[0] + gate * _dot(o_ref[0], wout_ref[...].astype(BF16))
    y_ref[0] = (x * lax.rsqrt(jnp.mean(x * x, axis=-1, keepdims=True) + EPS)) * g_ref[...]


def _out_layer(o, x, modb, w_out, g):
    bsz, s, d = x.shape
    tm = ROW_TILE
    row = lambda b, i: (b, i, 0)
    return pl.pallas_call(
        _out_kernel,
        grid=(bsz, s // tm),
        in_specs=[
            pl.BlockSpec((1, tm, d), row),
            pl.BlockSpec((1, tm, d), row),
            pl.BlockSpec((1, 3, d), lambda b, i: (b, 0, 0)),
            pl.BlockSpec((d, d), lambda b, i: (0, 0)),
            pl.BlockSpec((1, d), lambda b, i: (0, 0)),
        ],
        out_specs=pl.BlockSpec((1, tm, d), row),
        out_shape=jax.ShapeDtypeStruct((bsz, s, d), F32),
        compiler_params=pltpu.CompilerParams(
            dimension_semantics=("arbitrary", "arbitrary"),
            vmem_limit_bytes=_vmem_limit(1 / 2)),
        name="out_layer",
    )(o, x, modb, w_out, g)


def kernel(x, c, a_mod_w, a_mod_b, a_norm_g, a_w_in, a_conv_w, a_w_out,
           kv_mod_w, kv_mod_b, kv_norm_g, w_kv,
           b_mod_w, b_mod_b, b_norm_g, b_w_qz, b_w_out, final_norm_g):
    bsz, _, d = x.shape
    assert d == D_MODEL and a_mod_w.shape[0] == 1 and b_mod_w.shape[0] == 1

    mod = _modulation(c, [a_mod_w.reshape(d, 3 * d), kv_mod_w, b_mod_w.reshape(d, 3 * d)],
                      [a_mod_b, kv_mod_b, b_mod_b])
    mod_a = mod[:, :3 * d].reshape(bsz, 3, d)
    mod_kv = mod[:, 3 * d:5 * d].reshape(bsz, 2, d)
    mod_b = mod[:, 5 * d:].reshape(bsz, 3, d)

    x1 = _conv_layer(x, mod_a, a_norm_g, a_w_in.reshape(d, 4 * d), a_conv_w.reshape(CONV_K, d),
                     a_w_out.reshape(d, d))
    q, kv, z = _projections(x1, mod_kv, mod_b, kv_norm_g.reshape(1, d), b_norm_g,
                            w_kv, b_w_qz.reshape(d, 2 * d))
    o = _attention(q, kv, z)
    return _out_layer(o, x1, mod_b, b_w_out.reshape(d, d), final_norm_g.reshape(1, d))
```

```python
import functools
import math

import jax
import jax.numpy as jnp
from jax import lax
from jax.experimental import pallas as pl
from jax.experimental.pallas import tpu as pltpu

D_MODEL = 1024
N_HEADS = 16
HEAD_DIM = D_MODEL // N_HEADS
CONV_K = 3
EPS = 1e-6
LOG2E = 1.4426950408889634

LANES = 128
SUBLANES = 8
MXU_DIM = 256

VMEM_BYTES = 64 * 1024 * 1024


def _vmem_limit(fraction):
    return int(VMEM_BYTES * fraction)


ROW_TILE = 1024
MOD_COL_TILE = 1024
HEAD_GROUPS_PER_STEP = 4

K_BLOCK = MXU_DIM
Q_TILE = K_BLOCK
NEGLIGIBLE_LOG2 = -135.0
CONV_ROW_CHUNKS = 4
LOGITS_LEAD = 2

BF16 = jnp.bfloat16
F32 = jnp.float32


def _dot(a, b):
    return jnp.dot(a, b, preferred_element_type=F32)


def _silu(x):
    return x * jax.nn.sigmoid(x)


def _mod_kernel(c_ref, *refs, starts):
    n = len(starts) - 1
    w_refs, b_refs, o_ref = refs[:n], refs[n:2 * n], refs[2 * n]
    j = pl.program_id(0)
    s = _silu(c_ref[...]).astype(BF16)
    for i in range(n):
        @pl.when(jnp.logical_and(j >= starts[i], j < starts[i + 1]))
        def _(i=i):
            o_ref[...] = _dot(s, w_refs[i][...].astype(BF16)) + b_refs[i][...]


def _modulation(c, ws, bs):
    bsz, d = c.shape
    tn = MOD_COL_TILE
    tiles = [w.shape[1] // tn for w in ws]
    starts = [sum(tiles[:i]) for i in range(len(ws) + 1)]

    def tile_of(i):
        return lambda j: (0, jnp.clip(j - starts[i], 0, tiles[i] - 1))

    return pl.pallas_call(
        functools.partial(_mod_kernel, starts=tuple(starts)),
        grid=(starts[-1],),
        in_specs=([pl.BlockSpec((bsz, d), lambda j: (0, 0))]
                  + [pl.BlockSpec((d, tn), tile_of(i)) for i in range(len(ws))]
                  + [pl.BlockSpec((1, tn), tile_of(i)) for i in range(len(ws))]),
        out_specs=pl.BlockSpec((bsz, tn), lambda j: (0, j)),
        out_shape=jax.ShapeDtypeStruct((bsz, starts[-1] * tn), F32),
        compiler_params=pltpu.CompilerParams(
            dimension_semantics=("arbitrary",),
            vmem_limit_bytes=_vmem_limit(5 / 8)),
        name="modulation",
    )(c, *ws, *[b.reshape(1, -1) for b in bs])


def _conv_layer_kernel(x_ref, mod_ref, g_ref, win_ref, cw_ref, wout_ref, o_ref, carry_ref):
    tm = x_ref.shape[1]
    d = x_ref.shape[2]

    @pl.when(pl.program_id(1) == 0)
    def _():
        carry_ref[...] = jnp.zeros_like(carry_ref)

    mod = mod_ref[0]
    shift, scale, gate = mod[0:1], mod[1:2], mod[2:3]
    gain = g_ref[...] * (1.0 + scale)
    cw = cw_ref[...]
    w_in = [win_ref[:, p * d:(p + 1) * d].astype(BF16) for p in range(4)]
    w_out = wout_ref[...].astype(BF16)

    rows = tm // CONV_ROW_CHUNKS
    tail = carry_ref[...]
    for r0 in range(0, tm, rows):
        x = x_ref[0, r0:r0 + rows, :]
        rs = lax.rsqrt(jnp.mean(x * x, axis=-1, keepdims=True) + EPS)
        h = ((x * rs) * gain + shift).astype(BF16)
        b_gate, c_gate, u, z = (_dot(h, w) for w in w_in)

        cu = c_gate * u
        ext = jnp.concatenate([tail, cu], axis=0)
        prev1 = pltpu.roll(ext, 1, 0)[SUBLANES:]
        prev2 = pltpu.roll(ext, 2, 0)[SUBLANES:]
        tail = cu[rows - SUBLANES:]
        conv = cw[0:1] * prev2 + cw[1:2] * prev1 + cw[2:3] * cu

        y = (b_gate * conv) * _silu(z)
        o_ref[0, r0:r0 + rows, :] = x + gate * _dot(y.astype(BF16), w_out)
    carry_ref[...] = tail


def _conv_layer(x, mod, g, w_in, conv_w, w_out):
    bsz, s, d = x.shape
    tm = ROW_TILE
    const = dict(pipeline_mode=pl.Buffered(1))
    return pl.pallas_call(
        _conv_layer_kernel,
        grid=(bsz, s // tm),
        in_specs=[
            pl.BlockSpec((1, tm, d), lambda b, i: (b, i, 0)),
            pl.BlockSpec((1, 3, d), lambda b, i: (b, 0, 0)),
            pl.BlockSpec((1, d), lambda b, i: (0, 0)),
            pl.BlockSpec((d, 4 * d), lambda b, i: (0, 0), **const),
            pl.BlockSpec((CONV_K, d), lambda b, i: (0, 0)),
            pl.BlockSpec((d, d), lambda b, i: (0, 0), **const),
        ],
        out_specs=pl.BlockSpec((1, tm, d), lambda b, i: (b, i, 0)),
        out_shape=jax.ShapeDtypeStruct((bsz, s, d), F32),
        scratch_shapes=[pltpu.VMEM((SUBLANES, d), F32)],
        compiler_params=pltpu.CompilerParams(
            dimension_semantics=("arbitrary", "arbitrary"),
            vmem_limit_bytes=_vmem_limit(7 / 8)),
        name="conv_layer",
    )(x, mod, g, w_in, conv_w, w_out)


def _proj_kernel(x_ref, modkv_ref, modb_ref, gkv_ref, gb_ref, wkv_ref, wqz_ref,
                 q_ref, kv_ref, z_ref):
    d = x_ref.shape[2]
    x = x_ref[0]
    xn = x * lax.rsqrt(jnp.mean(x * x, axis=-1, keepdims=True) + EPS)
    modkv = modkv_ref[0]
    modb = modb_ref[0]
    hkv = (xn * (gkv_ref[...] * (1.0 + modkv[1:2])) + modkv[0:1]).astype(BF16)
    hq = (xn * (gb_ref[...] * (1.0 + modb[1:2])) + modb[0:1]).astype(BF16)
    kv_ref[0] = _dot(hkv, wkv_ref[...].astype(BF16)).astype(BF16)
    q_ref[0] = (_dot(hq, wqz_ref[:, :d].astype(BF16)) * (LOG2E / math.sqrt(HEAD_DIM))).astype(BF16)
    z_ref[0] = _dot(hq, wqz_ref[:, d:].astype(BF16))


def _projections(x, modkv, modb, gkv, gb, w_kv, w_qz):
    bsz, s, d = x.shape
    tm = ROW_TILE
    const = dict(pipeline_mode=pl.Buffered(1))
    row = lambda b, i: (b, i, 0)
    return pl.pallas_call(
        _proj_kernel,
        grid=(bsz, s // tm),
        in_specs=[
            pl.BlockSpec((1, tm, d), row),
            pl.BlockSpec((1, 2, d), lambda b, i: (b, 0, 0)),
            pl.BlockSpec((1, 3, d), lambda b, i: (b, 0, 0)),
            pl.BlockSpec((1, d), lambda b, i: (0, 0)),
            pl.BlockSpec((1, d), lambda b, i: (0, 0)),
            pl.BlockSpec((d, 2 * d), lambda b, i: (0, 0), **const),
            pl.BlockSpec((d, 2 * d), lambda b, i: (0, 0), **const),
        ],
        out_specs=[
            pl.BlockSpec((1, tm, d), row),
            pl.BlockSpec((1, tm, 2 * d), row),
            pl.BlockSpec((1, tm, d), row),
        ],
        out_shape=[
            jax.ShapeDtypeStruct((bsz, s, d), BF16),
            jax.ShapeDtypeStruct((bsz, s, 2 * d), BF16),
            jax.ShapeDtypeStruct((bsz, s, d), F32),
        ],
        compiler_params=pltpu.CompilerParams(
            dimension_semantics=("arbitrary", "arbitrary"),
            vmem_limit_bytes=_vmem_limit(7 / 8)),
        name="projections",
    )(x, modkv, modb, gkv, gb, w_kv, w_qz)


def _suffix_weights():
    j = lax.broadcasted_iota(jnp.int32, (K_BLOCK, K_BLOCK), 0)
    s = lax.broadcasted_iota(jnp.int32, (K_BLOCK, K_BLOCK), 1)
    return jnp.where((j > s) | (s == K_BLOCK - 1), -1.0, 0.0).astype(BF16)


def _attn_kernel(q_ref, k_ref, v_ref, z_ref, w_ref, o_ref,
                 qm_ref, acc_ref, carry_ref, z2a_ref, z2b_ref, aa_ref, ab_ref):
    n_q = q_ref.shape[1] // Q_TILE
    n_groups = q_ref.shape[2] // LANES
    heads = tuple(range(2 * n_groups))
    lane = lax.broadcasted_iota(jnp.int32, (1, LANES), 1)
    head0 = lane < HEAD_DIM
    last_key = lax.broadcasted_iota(jnp.int32, (Q_TILE, K_BLOCK), 1) == K_BLOCK - 1
    quadrant = (Q_TILE // 2, K_BLOCK // 2)
    causal_q = (lax.broadcasted_iota(jnp.int32, quadrant, 1)
                < lax.broadcasted_iota(jnp.int32, quadrant, 0))
    w = w_ref[...]

    def block_start(kb):
        return pl.multiple_of(kb * K_BLOCK, K_BLOCK)

    def logits(h, kb, z2_buf):
        g = h // 2
        k_blk = k_ref[0, pl.ds(block_start(kb), K_BLOCK), g * LANES:(g + 1) * LANES]
        z2_buf[h] = lax.dot_general(qm_ref[h], k_blk, (((1,), (1,)), ((), ())),
                                    preferred_element_type=F32)

    def accumulate(h, a_buf, kb):
        g = h // 2
        v_blk = v_ref[0, pl.ds(block_start(kb), K_BLOCK), g * LANES:(g + 1) * LANES]
        acc_ref[h] += _dot(a_buf[h], v_blk)

    half = Q_TILE // 2

    def softplus2(z2):
        return jnp.maximum(z2, 0.0) + jnp.log(1.0 + jnp.exp2(-jnp.abs(z2))) * LOG2E

    def suffix_sums(h, z2_buf, masked):
        if not masked:
            z2 = z2_buf[h]
            sp = softplus2(z2)
            z2_buf[h] = z2 - sp
            return _dot(sp.astype(BF16), w)
        z_tl = z2_buf[h, :half, :half]
        sp_tl = softplus2(z_tl)
        z2_buf[h, :half, :half] = z_tl - sp_tl
        z_bot = z2_buf[h, half:, :]
        sp_bot = softplus2(z_bot)
        z2_buf[h, half:, :] = z_bot - sp_bot
        sp_top = jnp.concatenate([jnp.where(causal_q, sp_tl, 0.0), jnp.zeros_like(sp_tl)], axis=1)
        sp_bot = jnp.concatenate([sp_bot[:, :half], jnp.where(causal_q, sp_bot[:, half:], 0.0)],
                                 axis=1)
        return _dot(jnp.concatenate([sp_top, sp_bot], axis=0).astype(BF16), w)

    def weights(h, r, z2_buf, a_buf, masked):
        total = jnp.broadcast_to(r[:, K_BLOCK - 1:], (Q_TILE, LANES))
        if not masked:
            carry = carry_ref[h]
            nxt = carry + total
            carry_ref[h] = nxt
            suffix = jnp.where(last_key, 0.0, r)
            ah = jnp.exp2(z2_buf[h] + (suffix + jnp.concatenate([carry] * (K_BLOCK // LANES), axis=1)))
            a_buf[h] = ah.astype(BF16)
            return nxt
        carry_ref[h] = total
        a_tl = jnp.exp2(z2_buf[h, :half, :half] + r[:half, :half])
        a_bl = jnp.exp2(z2_buf[h, half:, :half] + r[half:, :half])
        a_br = jnp.exp2(z2_buf[h, half:, half:] + r[half:, half:])
        a_top = jnp.concatenate([jnp.where(causal_q, a_tl, 0.0), jnp.zeros_like(a_tl)], axis=1)
        a_bot = jnp.concatenate([a_bl, jnp.where(causal_q, a_br, 0.0)], axis=1)
        a_buf[h] = jnp.concatenate([a_top, a_bot], axis=0).astype(BF16)
        return total

    def block_step(kb, kb_prev, variant, masked):
        z2_cur, z2_next = (z2a_ref, z2b_ref) if variant == 0 else (z2b_ref, z2a_ref)
        a_cur, a_prev = (aa_ref, ab_ref) if variant == 0 else (ab_ref, aa_ref)
        if masked:
            for h in heads[:LOGITS_LEAD]:
                logits(h, kb, z2_cur)
        worst = None
        for i, h in enumerate(heads):
            if masked and i + LOGITS_LEAD < len(heads):
                logits(heads[i + LOGITS_LEAD], kb, z2_cur)
            r = suffix_sums(h, z2_cur, masked)
            logits(h, jnp.maximum(kb - 1, 0), z2_next)
            accumulate(h, a_prev, kb_prev)
            nxt = weights(h, r, z2_cur, a_cur, masked)
            if not masked:
                worst = nxt if worst is None else jnp.maximum(worst, nxt)
        return None if masked else jnp.max(worst)

    def alternate(variant, fn):
        return lax.cond(variant == 0, lambda: fn(0), lambda: fn(1))

    def finalize(qstart):
        for g in range(n_groups):
            lanes = slice(g * LANES, (g + 1) * LANES)
            zg = z_ref[0, pl.ds(qstart, Q_TILE), lanes]
            acc = jnp.where(head0, acc_ref[2 * g], acc_ref[2 * g + 1])
            o_ref[0, pl.ds(qstart, Q_TILE), lanes] = (acc * _silu(zg)).astype(o_ref.dtype)
        acc_ref[...] = jnp.zeros_like(acc_ref)

    def split_heads(qstart):
        for g in range(n_groups):
            q = q_ref[0, pl.ds(qstart, Q_TILE), g * LANES:(g + 1) * LANES]
            zero = jnp.zeros_like(q)
            qm_ref[2 * g] = jnp.where(head0, q, zero)
            qm_ref[2 * g + 1] = jnp.where(head0, zero, q)

    def q_tile(qi, state):
        variant, kb_prev = state

        def diagonal_step(v):
            split_heads(pl.multiple_of(qi * Q_TILE, Q_TILE))
            block_step(qi, kb_prev, v, True)
            finalize(pl.multiple_of((qi - 1) * Q_TILE, Q_TILE))

        alternate(variant, diagonal_step)

        def more_blocks(state):
            j, worst, _ = state
            return jnp.logical_and(j < qi, worst > NEGLIGIBLE_LOG2)

        def below_diagonal(state):
            j, _, variant = state
            kb = qi - 1 - j
            worst = alternate(variant, lambda v: block_step(kb, kb + 1, v, False))
            return j + 1, worst, 1 - variant

        n_below, _, variant = lax.while_loop(more_blocks, below_diagonal,
                                             (jnp.int32(0), jnp.float32(0.0), 1 - variant))
        return variant, qi - n_below

    first = jnp.int32(0)
    split_heads(0)
    for h in heads:
        logits(h, first, z2a_ref)
    for h in heads:
        weights(h, suffix_sums(h, z2a_ref, True), z2a_ref, aa_ref, True)
    acc_ref[...] = jnp.zeros_like(acc_ref)
    variant, kb_prev = lax.fori_loop(1, n_q, q_tile, (jnp.int32(1), first))

    def drain(v):
        a_prev = ab_ref if v == 0 else aa_ref
        for h in heads:
            accumulate(h, a_prev, kb_prev)

    alternate(variant, drain)
    finalize((n_q - 1) * Q_TILE)


def _attention(q, kv, z):
    bsz, s, d = q.shape
    width = HEAD_GROUPS_PER_STEP * LANES
    n_steps = d // width
    n_heads = 2 * HEAD_GROUPS_PER_STEP
    grp = lambda b, g: (b, 0, g)
    return pl.pallas_call(
        _attn_kernel,
        grid=(bsz, n_steps),
        in_specs=[
            pl.BlockSpec((1, s, width), grp),
            pl.BlockSpec((1, s, width), grp),
            pl.BlockSpec((1, s, width), lambda b, g: (b, 0, n_steps + g)),
            pl.BlockSpec((1, s, width), grp),
            pl.BlockSpec((K_BLOCK, K_BLOCK), lambda b, g: (0, 0)),
        ],
        out_specs=pl.BlockSpec((1, s, width), grp),
        out_shape=jax.ShapeDtypeStruct((bsz, s, d), BF16),
        scratch_shapes=[
            pltpu.VMEM((n_heads, Q_TILE, LANES), BF16),
            pltpu.VMEM((n_heads, Q_TILE, LANES), F32),
            pltpu.VMEM((n_heads, Q_TILE, LANES), F32),
            pltpu.VMEM((n_heads, Q_TILE, K_BLOCK), F32),
            pltpu.VMEM((n_heads, Q_TILE, K_BLOCK), F32),
            pltpu.VMEM((n_heads, Q_TILE, K_BLOCK), BF16),
            pltpu.VMEM((n_heads, Q_TILE, K_BLOCK), BF16),
        ],
        compiler_params=pltpu.CompilerParams(
            dimension_semantics=("arbitrary", "arbitrary"),
            vmem_limit_bytes=_vmem_limit(5 / 8)),
        name="stickbreak_attention",
    )(q, kv, kv, z, _suffix_weights())


def _out_kernel(o_ref, x_ref, mod_ref, wout_ref, g_ref, y_ref):
    gate = mod_ref[0][2:3]
    x = x_ref[0] + gate * _dot(o_ref[0], wout_ref[...].astype(BF16))
    y_ref[0] = (x * lax.rsqrt(jnp.mean(x * x, axis=-1, keepdims=True) + EPS)) * g_ref[...]


def _out_layer(o, x, modb, w_out, g):
    bsz, s, d = x.shape
    tm = ROW_TILE
    row = lambda b, i: (b, i, 0)
    return pl.pallas_call(
        _out_kernel,
        grid=(bsz, s // tm),
        in_specs=[
            pl.BlockSpec((1, tm, d), row),
            pl.BlockSpec((1, tm, d), row),
            pl.BlockSpec((1, 3, d), lambda b, i: (b, 0, 0)),
            pl.BlockSpec((d, d), lambda b, i: (0, 0)),
            pl.BlockSpec((1, d), lambda b, i: (0, 0)),
        ],
        out_specs=pl.BlockSpec((1, tm, d), row),
        out_shape=jax.ShapeDtypeStruct((bsz, s, d), F32),
        compiler_params=pltpu.CompilerParams(
            dimension_semantics=("arbitrary", "arbitrary"),
            vmem_limit_bytes=_vmem_limit(1 / 2)),
        name="out_layer",
    )(o, x, modb, w_out, g)


def kernel(x, c, a_mod_w, a_mod_b, a_norm_g, a_w_in, a_conv_w, a_w_out,
           kv_mod_w, kv_mod_b, kv_norm_g, w_kv,
           b_mod_w, b_mod_b, b_norm_g, b_w_qz, b_w_out, final_norm_g):
    bsz, _, d = x.shape
    assert d == D_MODEL and a_mod_w.shape[0] == 1 and b_mod_w.shape[0] == 1

    mod = _modulation(c, [a_mod_w.reshape(d, 3 * d), kv_mod_w, b_mod_w.reshape(d, 3 * d)],
                      [a_mod_b, kv_mod_b, b_mod_b])
    mod_a = mod[:, :3 * d].reshape(bsz, 3, d)
    mod_kv = mod[:, 3 * d:5 * d].reshape(bsz, 2, d)
    mod_b = mod[:, 5 * d:].reshape(bsz, 3, d)

    x1 = _conv_layer(x, mod_a, a_norm_g, a_w_in[0], a_conv_w[0], a_w_out[0])
    q, kv, z = _projections(x1, mod_kv, mod_b, kv_norm_g.reshape(1, d), b_norm_g,
                            w_kv, b_w_qz[0])
    o = _attention(q, kv, z)
    return _out_layer(o, x1, mod_b, b_w_out[0], final_norm_g.reshape(1, d))
```

```python
import functools
import math

import jax
import jax.numpy as jnp
from jax import lax
from jax.experimental import pallas as pl
from jax.experimental.pallas import tpu as pltpu

D_MODEL = 1024
N_HEADS = 16
HEAD_DIM = D_MODEL // N_HEADS
CONV_K = 3
EPS = 1e-6
LOG2E = 1.4426950408889634

LANES = 128
SUBLANES = 8
MXU_DIM = 256

VMEM_BYTES = 64 * 1024 * 1024


def _vmem_limit(fraction):
    return int(VMEM_BYTES * fraction)


ROW_TILE = 1024
MOD_COL_TILE = 1024
HEAD_GROUPS_PER_STEP = 4

K_BLOCK = MXU_DIM
Q_TILE = K_BLOCK
NEGLIGIBLE_LOG2 = -135.0
CONV_ROW_CHUNKS = 4
LOGITS_LEAD = 2

BF16 = jnp.bfloat16
F32 = jnp.float32


def _dot(a, b):
    return jnp.dot(a, b, preferred_element_type=F32)


def _silu(x):
    return x * jax.nn.sigmoid(x)


def _mod_kernel(c_ref, *refs, starts):
    n = len(starts) - 1
    w_refs, b_refs, o_ref = refs[:n], refs[n:2 * n], refs[2 * n]
    j = pl.program_id(0)
    s = _silu(c_ref[...]).astype(BF16)
    for i in range(n):
        @pl.when(jnp.logical_and(j >= starts[i], j < starts[i + 1]))
        def _(i=i):
            o_ref[...] = _dot(s, w_refs[i][...].astype(BF16)) + b_refs[i][...]


def _modulation(c, ws, bs):
    bsz, d = c.shape
    tn = MOD_COL_TILE
    tiles = [w.shape[1] // tn for w in ws]
    starts = [sum(tiles[:i]) for i in range(len(ws) + 1)]

    def tile_of(i):
        return lambda j: (0, jnp.clip(j - starts[i], 0, tiles[i] - 1))

    return pl.pallas_call(
        functools.partial(_mod_kernel, starts=tuple(starts)),
        grid=(starts[-1],),
        in_specs=([pl.BlockSpec((bsz, d), lambda j: (0, 0))]
                  + [pl.BlockSpec((d, tn), tile_of(i)) for i in range(len(ws))]
                  + [pl.BlockSpec((1, tn), tile_of(i)) for i in range(len(ws))]),
        out_specs=pl.BlockSpec((bsz, tn), lambda j: (0, j)),
        out_shape=jax.ShapeDtypeStruct((bsz, starts[-1] * tn), F32),
        compiler_params=pltpu.CompilerParams(
            dimension_semantics=("arbitrary",),
            vmem_limit_bytes=_vmem_limit(5 / 8)),
        name="modulation",
    )(c, *ws, *[b.reshape(1, -1) for b in bs])


def _conv_layer_kernel(x_ref, mod_ref, g_ref, win_ref, cw_ref, wout_ref, o_ref, carry_ref):
    tm = x_ref.shape[1]
    d = x_ref.shape[2]

    @pl.when(pl.program_id(1) == 0)
    def _():
        carry_ref[...] = jnp.zeros_like(carry_ref)

    mod = mod_ref[0]
    shift, scale, gate = mod[0:1], mod[1:2], mod[2:3]
    gain = g_ref[...] * (1.0 + scale)
    cw = cw_ref[...]
    w_in = [win_ref[:, p * d:(p + 1) * d].astype(BF16) for p in range(4)]
    w_out = wout_ref[...].astype(BF16)

    rows = tm // CONV_ROW_CHUNKS
    tail = carry_ref[...]
    for r0 in range(0, tm, rows):
        x = x_ref[0, r0:r0 + rows, :]
        rs = lax.rsqrt(jnp.mean(x * x, axis=-1, keepdims=True) + EPS)
        h = ((x * rs) * gain + shift).astype(BF16)
        b_gate, c_gate, u, z = (_dot(h, w) for w in w_in)

        cu = c_gate * u
        ext = jnp.concatenate([tail, cu], axis=0)
        prev1 = pltpu.roll(ext, 1, 0)[SUBLANES:]
        prev2 = pltpu.roll(ext, 2, 0)[SUBLANES:]
        tail = cu[rows - SUBLANES:]
        conv = cw[0:1] * prev2 + cw[1:2] * prev1 + cw[2:3] * cu

        y = (b_gate * conv) * _silu(z)
        o_ref[0, r0:r0 + rows, :] = x + gate * _dot(y.astype(BF16), w_out)
    carry_ref[...] = tail


def _conv_layer(x, mod, g, w_in, conv_w, w_out):
    bsz, s, d = x.shape
    tm = ROW_TILE
    const = dict(pipeline_mode=pl.Buffered(1))
    return pl.pallas_call(
        _conv_layer_kernel,
        grid=(bsz, s // tm),
        in_specs=[
            pl.BlockSpec((1, tm, d), lambda b, i: (b, i, 0)),
            pl.BlockSpec((1, 3, d), lambda b, i: (b, 0, 0)),
            pl.BlockSpec((1, d), lambda b, i: (0, 0)),
            pl.BlockSpec((d, 4 * d), lambda b, i: (0, 0), **const),
            pl.BlockSpec((CONV_K, d), lambda b, i: (0, 0)),
            pl.BlockSpec((d, d), lambda b, i: (0, 0), **const),
        ],
        out_specs=pl.BlockSpec((1, tm, d), lambda b, i: (b, i, 0)),
        out_shape=jax.ShapeDtypeStruct((bsz, s, d), F32),
        scratch_shapes=[pltpu.VMEM((SUBLANES, d), F32)],
        compiler_params=pltpu.CompilerParams(
            dimension_semantics=("arbitrary", "arbitrary"),
            vmem_limit_bytes=_vmem_limit(7 / 8)),
        name="conv_layer",
    )(x, mod, g, w_in, conv_w, w_out)


def _proj_kernel(x_ref, modkv_ref, modb_ref, gkv_ref, gb_ref, wkv_ref, wqz_ref,
                 q_ref, kv_ref, z_ref):
    d = x_ref.shape[2]
    x = x_ref[0]
    xn = x * lax.rsqrt(jnp.mean(x * x, axis=-1, keepdims=True) + EPS)
    modkv = modkv_ref[0]
    modb = modb_ref[0]
    hkv = (xn * (gkv_ref[...] * (1.0 + modkv[1:2])) + modkv[0:1]).astype(BF16)
    hq = (xn * (gb_ref[...] * (1.0 + modb[1:2])) + modb[0:1]).astype(BF16)
    kv_ref[0] = _dot(hkv, wkv_ref[...].astype(BF16)).astype(BF16)
    q_ref[0] = (_dot(hq, wqz_ref[:, :d].astype(BF16)) * (LOG2E / math.sqrt(HEAD_DIM))).astype(BF16)
    z_ref[0] = _dot(hq, wqz_ref[:, d:].astype(BF16))


def _projections(x, modkv, modb, gkv, gb, w_kv, w_qz):
    bsz, s, d = x.shape
    tm = ROW_TILE
    const = dict(pipeline_mode=pl.Buffered(1))
    row = lambda b, i: (b, i, 0)
    return pl.pallas_call(
        _proj_kernel,
        grid=(bsz, s // tm),
        in_specs=[
            pl.BlockSpec((1, tm, d), row),
            pl.BlockSpec((1, 2, d), lambda b, i: (b, 0, 0)),
            pl.BlockSpec((1, 3, d), lambda b, i: (b, 0, 0)),
            pl.BlockSpec((1, d), lambda b, i: (0, 0)),
            pl.BlockSpec((1, d), lambda b, i: (0, 0)),
            pl.BlockSpec((d, 2 * d), lambda b, i: (0, 0), **const),
            pl.BlockSpec((d, 2 * d), lambda b, i: (0, 0), **const),
        ],
        out_specs=[
            pl.BlockSpec((1, tm, d), row),
            pl.BlockSpec((1, tm, 2 * d), row),
            pl.BlockSpec((1, tm, d), row),
        ],
        out_shape=[
            jax.ShapeDtypeStruct((bsz, s, d), BF16),
            jax.ShapeDtypeStruct((bsz, s, 2 * d), BF16),
            jax.ShapeDtypeStruct((bsz, s, d), F32),
        ],
        compiler_params=pltpu.CompilerParams(
            dimension_semantics=("arbitrary", "arbitrary"),
            vmem_limit_bytes=_vmem_limit(7 / 8)),
        name="projections",
    )(x, modkv, modb, gkv, gb, w_kv, w_qz)


def _suffix_weights():
    j = lax.broadcasted_iota(jnp.int32, (K_BLOCK, K_BLOCK), 0)
    s = lax.broadcasted_iota(jnp.int32, (K_BLOCK, K_BLOCK), 1)
    return jnp.where((j > s) | (s == K_BLOCK - 1), -1.0, 0.0).astype(BF16)


def _attn_kernel(q_ref, k_ref, v_ref, z_ref, w_ref, o_ref,
                 qm_ref, acc_ref, carry_ref, z2a_ref, z2b_ref, aa_ref, ab_ref):
    n_q = q_ref.shape[1] // Q_TILE
    n_groups = q_ref.shape[2] // LANES
    heads = tuple(range(2 * n_groups))
    lane = lax.broadcasted_iota(jnp.int32, (1, LANES), 1)
    head0 = lane < HEAD_DIM
    last_key = lax.broadcasted_iota(jnp.int32, (Q_TILE, K_BLOCK), 1) == K_BLOCK - 1
    quadrant = (Q_TILE // 2, K_BLOCK // 2)
    causal_q = (lax.broadcasted_iota(jnp.int32, quadrant, 1)
                < lax.broadcasted_iota(jnp.int32, quadrant, 0))
    w = w_ref[...]

    def block_start(kb):
        return pl.multiple_of(kb * K_BLOCK, K_BLOCK)

    def logits(h, kb, z2_buf):
        g = h // 2
        k_blk = k_ref[0, pl.ds(block_start(kb), K_BLOCK), g * LANES:(g + 1) * LANES]
        z2_buf[h] = lax.dot_general(qm_ref[h], k_blk, (((1,), (1,)), ((), ())),
                                    preferred_element_type=F32)

    def accumulate(h, a_buf, kb):
        g = h // 2
        v_blk = v_ref[0, pl.ds(block_start(kb), K_BLOCK), g * LANES:(g + 1) * LANES]
        acc_ref[h] += _dot(a_buf[h], v_blk)

    half = Q_TILE // 2

    def softplus2(z2):
        return jnp.maximum(z2, 0.0) + jnp.log(1.0 + jnp.exp2(-jnp.abs(z2))) * LOG2E

    def suffix_sums(h, z2_buf, masked):
        if not masked:
            z2 = z2_buf[h]
            sp = softplus2(z2)
            z2_buf[h] = z2 - sp
            return _dot(sp.astype(BF16), w)
        z_tl = z2_buf[h, :half, :half]
        sp_tl = softplus2(z_tl)
        z2_buf[h, :half, :half] = z_tl - sp_tl
        z_bot = z2_buf[h, half:, :]
        sp_bot = softplus2(z_bot)
        z2_buf[h, half:, :] = z_bot - sp_bot
        sp_top = jnp.concatenate([jnp.where(causal_q, sp_tl, 0.0), jnp.zeros_like(sp_tl)], axis=1)
        sp_bot = jnp.concatenate([sp_bot[:, :half], jnp.where(causal_q, sp_bot[:, half:], 0.0)],
                                 axis=1)
        return _dot(jnp.concatenate([sp_top, sp_bot], axis=0).astype(BF16), w)

    def weights(h, r, z2_buf, a_buf, masked):
        total = jnp.broadcast_to(r[:, K_BLOCK - 1:], (Q_TILE, LANES))
        if not masked:
            carry = carry_ref[h]
            nxt = carry + total
            carry_ref[h] = nxt
            suffix = jnp.where(last_key, 0.0, r)
            ah = jnp.exp2(z2_buf[h] + (suffix + jnp.concatenate([carry] * (K_BLOCK // LANES), axis=1)))
            a_buf[h] = ah.astype(BF16)
            return nxt
        carry_ref[h] = total
        a_tl = jnp.exp2(z2_buf[h, :half, :half] + r[:half, :half])
        a_bl = jnp.exp2(z2_buf[h, half:, :half] + r[half:, :half])
        a_br = jnp.exp2(z2_buf[h, half:, half:] + r[half:, half:])
        a_top = jnp.concatenate([jnp.where(causal_q, a_tl, 0.0), jnp.zeros_like(a_tl)], axis=1)
        a_bot = jnp.concatenate([a_bl, jnp.where(causal_q, a_br, 0.0)], axis=1)
        a_buf[h] = jnp.concatenate([a_top, a_bot], axis=0).astype(BF16)
        return total

    def block_step(kb, kb_prev, variant, masked):
        z2_cur, z2_next = (z2a_ref, z2b_ref) if variant == 0 else (z2b_ref, z2a_ref)
        a_cur, a_prev = (aa_ref, ab_ref) if variant == 0 else (ab_ref, aa_ref)
        if masked:
            for h in heads[:LOGITS_LEAD]:
                logits(h, kb, z2_cur)
        worst = None
        for i, h in enumerate(heads):
            if masked and i + LOGITS_LEAD < len(heads):
                logits(heads[i + LOGITS_LEAD], kb, z2_cur)
            r = suffix_sums(h, z2_cur, masked)
            logits(h, jnp.maximum(kb - 1, 0), z2_next)
            accumulate(h, a_prev, kb_prev)
            nxt = weights(h, r, z2_cur, a_cur, masked)
            if not masked:
                worst = nxt if worst is None else jnp.maximum(worst, nxt)
        return None if masked else jnp.max(worst)

    def alternate(variant, fn):
        return lax.cond(variant == 0, lambda: fn(0), lambda: fn(1))

    def finalize(qstart):
        for g in range(n_groups):
            lanes = slice(g * LANES, (g + 1) * LANES)
            zg = z_ref[0, pl.ds(qstart, Q_TILE), lanes]
            acc = jnp.where(head0, acc_ref[2 * g], acc_ref[2 * g + 1])
            o_ref[0, pl.ds(qstart, Q_TILE), lanes] = (acc * _silu(zg)).astype(o_ref.dtype)
        acc_ref[...] = jnp.zeros_like(acc_ref)

    def split_heads(qstart):
        for g in range(n_groups):
            q = q_ref[0, pl.ds(qstart, Q_TILE), g * LANES:(g + 1) * LANES]
            zero = jnp.zeros_like(q)
            qm_ref[2 * g] = jnp.where(head0, q, zero)
            qm_ref[2 * g + 1] = jnp.where(head0, zero, q)

    def q_tile(qi, state):
        variant, kb_prev = state

        def diagonal_step(v):
            split_heads(pl.multiple_of(qi * Q_TILE, Q_TILE))
            block_step(qi, kb_prev, v, True)
            finalize(pl.multiple_of((qi - 1) * Q_TILE, Q_TILE))

        alternate(variant, diagonal_step)

        def more_blocks(state):
            j, worst, _ = state
            return jnp.logical_and(j < qi, worst > NEGLIGIBLE_LOG2)

        def below_diagonal(state):
            j, _, variant = state
            kb = qi - 1 - j
            worst = alternate(variant, lambda v: block_step(kb, kb + 1, v, False))
            return j + 1, worst, 1 - variant

        n_below, _, variant = lax.while_loop(more_blocks, below_diagonal,
                                             (jnp.int32(0), jnp.float32(0.0), 1 - variant))
        return variant, qi - n_below

    first = jnp.int32(0)
    split_heads(0)
    for h in heads:
        logits(h, first, z2a_ref)
    for h in heads:
        weights(h, suffix_sums(h, z2a_ref, True), z2a_ref, aa_ref, True)
    acc_ref[...] = jnp.zeros_like(acc_ref)
    variant, kb_prev = lax.fori_loop(1, n_q, q_tile, (jnp.int32(1), first))

    def drain(v):
        a_prev = ab_ref if v == 0 else aa_ref
        for h in heads:
            accumulate(h, a_prev, kb_prev)

    alternate(variant, drain)
    finalize((n_q - 1) * Q_TILE)


def _attention(q, kv, z):
    bsz, s, d = q.shape
    width = HEAD_GROUPS_PER_STEP * LANES
    n_steps = d // width
    n_heads = 2 * HEAD_GROUPS_PER_STEP
    grp = lambda b, g: (b, 0, g)
    return pl.pallas_call(
        _attn_kernel,
        grid=(bsz, n_steps),
        in_specs=[
            pl.BlockSpec((1, s, width), grp),
            pl.BlockSpec((1, s, width), grp),
            pl.BlockSpec((1, s, width), lambda b, g: (b, 0, n_steps + g)),
            pl.BlockSpec((1, s, width), grp),
            pl.BlockSpec((K_BLOCK, K_BLOCK), lambda b, g: (0, 0)),
        ],
        out_specs=pl.BlockSpec((1, s, width), grp),
        out_shape=jax.ShapeDtypeStruct((bsz, s, d), BF16),
        scratch_shapes=[
            pltpu.VMEM((n_heads, Q_TILE, LANES), BF16),
            pltpu.VMEM((n_heads, Q_TILE, LANES), F32),
            pltpu.VMEM((n_heads, Q_TILE, LANES), F32),
            pltpu.VMEM((n_heads, Q_TILE, K_BLOCK), F32),
            pltpu.VMEM((n_heads, Q_TILE, K_BLOCK), F32),
            pltpu.VMEM((n_heads, Q_TILE, K_BLOCK), BF16),
            pltpu.VMEM((n_heads, Q_TILE, K_BLOCK), BF16),
        ],
        compiler_params=pltpu.CompilerParams(
            dimension_semantics=("arbitrary", "arbitrary"),
            vmem_limit_bytes=_vmem_limit(5 / 8)),
        name="stickbreak_attention",
    )(q, kv, kv, z, _suffix_weights())


def _out_kernel(o_ref, x_ref, mod_ref, wout_ref, g_ref, y_ref):
    gate = mod_ref[0][2:3]
    w_out = wout_ref[...].astype(BF16)
    rows = x_ref.shape[1] // CONV_ROW_CHUNKS
    for r0 in range(0, x_ref.shape[1], rows):
        x = x_ref[0, r0:r0 + rows, :] + gate * _dot(o_ref[0, r0:r0 + rows, :], w_out)
        y_ref[0, r0:r0 + rows, :] = (
            x * lax.rsqrt(jnp.mean(x * x, axis=-1, keepdims=True) + EPS)) * g_ref[...]


def _out_layer(o, x, modb, w_out, g):
    bsz, s, d = x.shape
    tm = ROW_TILE
    row = lambda b, i: (b, i, 0)
    return pl.pallas_call(
        _out_kernel,
        grid=(bsz, s // tm),
        in_specs=[
            pl.BlockSpec((1, tm, d), row),
            pl.BlockSpec((1, tm, d), row),
            pl.BlockSpec((1, 3, d), lambda b, i: (b, 0, 0)),
            pl.BlockSpec((d, d), lambda b, i: (0, 0)),
            pl.BlockSpec((1, d), lambda b, i: (0, 0)),
        ],
        out_specs=pl.BlockSpec((1, tm, d), row),
        out_shape=jax.ShapeDtypeStruct((bsz, s, d), F32),
        compiler_params=pltpu.CompilerParams(
            dimension_semantics=("arbitrary", "arbitrary"),
            vmem_limit_bytes=_vmem_limit(1 / 2)),
        name="out_layer",
    )(o, x, modb, w_out, g)


def kernel(x, c, a_mod_w, a_mod_b, a_norm_g, a_w_in, a_conv_w, a_w_out,
           kv_mod_w, kv_mod_b, kv_norm_g, w_kv,
           b_mod_w, b_mod_b, b_norm_g, b_w_qz, b_w_out, final_norm_g):
    bsz, _, d = x.shape
    assert d == D_MODEL and a_mod_w.shape[0] == 1 and b_mod_w.shape[0] == 1

    mod = _modulation(c, [a_mod_w.reshape(d, 3 * d), kv_mod_w, b_mod_w.reshape(d, 3 * d)],
                      [a_mod_b, kv_mod_b, b_mod_b])
    mod_a = mod[:, :3 * d].reshape(bsz, 3, d)
    mod_kv = mod[:, 3 * d:5 * d].reshape(bsz, 2, d)
    mod_b = mod[:, 5 * d:].reshape(bsz, 3, d)

    x1 = _conv_layer(x, mod_a, a_norm_g, a_w_in[0], a_conv_w[0], a_w_out[0])
    q, kv, z = _projections(x1, mod_kv, mod_b, kv_norm_g.reshape(1, d), b_norm_g,
                            w_kv, b_w_qz[0])
    o = _attention(q, kv, z)
    return _out_layer(o, x1, mod_b, b_w_out[0], final_norm_g.reshape(1, d))
```

```python
import functools
import math

import jax
import jax.numpy as jnp
from jax import lax
from jax.experimental import pallas as pl
from jax.experimental.pallas import tpu as pltpu

D_MODEL = 1024
N_HEADS = 16
HEAD_DIM = D_MODEL // N_HEADS
CONV_K = 3
EPS = 1e-6
LOG2E = 1.4426950408889634

LANES = 128
SUBLANES = 8
MXU_DIM = 256

VMEM_BYTES = 64 * 1024 * 1024


def _vmem_limit(fraction):
    return int(VMEM_BYTES * fraction)


ROW_TILE = 1024
MOD_COL_TILE = 1024
HEAD_GROUPS_PER_STEP = 4

K_BLOCK = MXU_DIM
Q_TILE = K_BLOCK
NEGLIGIBLE_LOG2 = -135.0
CONV_ROW_CHUNKS = 4
OUT_RING = 3
LOGITS_LEAD = 2

BF16 = jnp.bfloat16
F32 = jnp.float32


def _dot(a, b):
    return jnp.dot(a, b, preferred_element_type=F32)


def _silu(x):
    return x * jax.nn.sigmoid(x)


def _mod_kernel(c_ref, *refs, starts):
    n = len(starts) - 1
    w_refs, b_refs, o_ref = refs[:n], refs[n:2 * n], refs[2 * n]
    j = pl.program_id(0)
    s = _silu(c_ref[...]).astype(BF16)
    for i in range(n):
        @pl.when(jnp.logical_and(j >= starts[i], j < starts[i + 1]))
        def _(i=i):
            o_ref[...] = _dot(s, w_refs[i][...].astype(BF16)) + b_refs[i][...]


def _modulation(c, ws, bs):
    bsz, d = c.shape
    tn = MOD_COL_TILE
    tiles = [w.shape[1] // tn for w in ws]
    starts = [sum(tiles[:i]) for i in range(len(ws) + 1)]

    def tile_of(i):
        return lambda j: (0, jnp.clip(j - starts[i], 0, tiles[i] - 1))

    return pl.pallas_call(
        functools.partial(_mod_kernel, starts=tuple(starts)),
        grid=(starts[-1],),
        in_specs=([pl.BlockSpec((bsz, d), lambda j: (0, 0))]
                  + [pl.BlockSpec((d, tn), tile_of(i)) for i in range(len(ws))]
                  + [pl.BlockSpec((1, tn), tile_of(i)) for i in range(len(ws))]),
        out_specs=pl.BlockSpec((bsz, tn), lambda j: (0, j)),
        out_shape=jax.ShapeDtypeStruct((bsz, starts[-1] * tn), F32),
        compiler_params=pltpu.CompilerParams(
            dimension_semantics=("arbitrary",),
            vmem_limit_bytes=_vmem_limit(5 / 8)),
        name="modulation",
    )(c, *ws, *[b.reshape(1, -1) for b in bs])


def _conv_layer_kernel(x_ref, mod_ref, g_ref, win_ref, cw_ref, wout_ref, o_ref, carry_ref):
    tm = x_ref.shape[1]
    d = x_ref.shape[2]

    @pl.when(pl.program_id(1) == 0)
    def _():
        carry_ref[...] = jnp.zeros_like(carry_ref)

    mod = mod_ref[0]
    shift, scale, gate = mod[0:1], mod[1:2], mod[2:3]
    gain = g_ref[...] * (1.0 + scale)
    cw = cw_ref[...]
    w_in = [win_ref[:, p * d:(p + 1) * d].astype(BF16) for p in range(4)]
    w_out = wout_ref[...].astype(BF16)

    rows = tm // CONV_ROW_CHUNKS
    tail = carry_ref[...]
    for r0 in range(0, tm, rows):
        x = x_ref[0, r0:r0 + rows, :]
        rs = lax.rsqrt(jnp.mean(x * x, axis=-1, keepdims=True) + EPS)
        h = ((x * rs) * gain + shift).astype(BF16)
        b_gate, c_gate, u, z = (_dot(h, w) for w in w_in)

        cu = c_gate * u
        ext = jnp.concatenate([tail, cu], axis=0)
        prev1 = pltpu.roll(ext, 1, 0)[SUBLANES:]
        prev2 = pltpu.roll(ext, 2, 0)[SUBLANES:]
        tail = cu[rows - SUBLANES:]
        conv = cw[0:1] * prev2 + cw[1:2] * prev1 + cw[2:3] * cu

        y = (b_gate * conv) * _silu(z)
        o_ref[0, r0:r0 + rows, :] = x + gate * _dot(y.astype(BF16), w_out)
    carry_ref[...] = tail


def _conv_layer(x, mod, g, w_in, conv_w, w_out):
    bsz, s, d = x.shape
    tm = ROW_TILE
    const = dict(pipeline_mode=pl.Buffered(1))
    return pl.pallas_call(
        _conv_layer_kernel,
        grid=(bsz, s // tm),
        in_specs=[
            pl.BlockSpec((1, tm, d), lambda b, i: (b, i, 0)),
            pl.BlockSpec((1, 3, d), lambda b, i: (b, 0, 0)),
            pl.BlockSpec((1, d), lambda b, i: (0, 0)),
            pl.BlockSpec((d, 4 * d), lambda b, i: (0, 0), **const),
            pl.BlockSpec((CONV_K, d), lambda b, i: (0, 0)),
            pl.BlockSpec((d, d), lambda b, i: (0, 0), **const),
        ],
        out_specs=pl.BlockSpec((1, tm, d), lambda b, i: (b, i, 0)),
        out_shape=jax.ShapeDtypeStruct((bsz, s, d), F32),
        scratch_shapes=[pltpu.VMEM((SUBLANES, d), F32)],
        compiler_params=pltpu.CompilerParams(
            dimension_semantics=("arbitrary", "arbitrary"),
            vmem_limit_bytes=_vmem_limit(7 / 8)),
        name="conv_layer",
    )(x, mod, g, w_in, conv_w, w_out)


def _proj_kernel(x_ref, modkv_ref, modb_ref, gkv_ref, gb_ref, wkv_ref, wqz_ref,
                 q_ref, kv_ref, z_ref):
    d = x_ref.shape[2]
    x = x_ref[0]
    xn = x * lax.rsqrt(jnp.mean(x * x, axis=-1, keepdims=True) + EPS)
    modkv = modkv_ref[0]
    modb = modb_ref[0]
    hkv = (xn * (gkv_ref[...] * (1.0 + modkv[1:2])) + modkv[0:1]).astype(BF16)
    hq = (xn * (gb_ref[...] * (1.0 + modb[1:2])) + modb[0:1]).astype(BF16)
    kv_ref[0] = _dot(hkv, wkv_ref[...].astype(BF16)).astype(BF16)
    q_ref[0] = (_dot(hq, wqz_ref[:, :d].astype(BF16)) * (LOG2E / math.sqrt(HEAD_DIM))).astype(BF16)
    z_ref[0] = _dot(hq, wqz_ref[:, d:].astype(BF16))


def _projections(x, modkv, modb, gkv, gb, w_kv, w_qz):
    bsz, s, d = x.shape
    tm = ROW_TILE
    const = dict(pipeline_mode=pl.Buffered(1))
    row = lambda b, i: (b, i, 0)
    return pl.pallas_call(
        _proj_kernel,
        grid=(bsz, s // tm),
        in_specs=[
            pl.BlockSpec((1, tm, d), row),
            pl.BlockSpec((1, 2, d), lambda b, i: (b, 0, 0)),
            pl.BlockSpec((1, 3, d), lambda b, i: (b, 0, 0)),
            pl.BlockSpec((1, d), lambda b, i: (0, 0)),
            pl.BlockSpec((1, d), lambda b, i: (0, 0)),
            pl.BlockSpec((d, 2 * d), lambda b, i: (0, 0), **const),
            pl.BlockSpec((d, 2 * d), lambda b, i: (0, 0), **const),
        ],
        out_specs=[
            pl.BlockSpec((1, tm, d), row),
            pl.BlockSpec((1, tm, 2 * d), row),
            pl.BlockSpec((1, tm, d), row),
        ],
        out_shape=[
            jax.ShapeDtypeStruct((bsz, s, d), BF16),
            jax.ShapeDtypeStruct((bsz, s, 2 * d), BF16),
            jax.ShapeDtypeStruct((bsz, s, d), F32),
        ],
        compiler_params=pltpu.CompilerParams(
            dimension_semantics=("arbitrary", "arbitrary"),
            vmem_limit_bytes=_vmem_limit(7 / 8)),
        name="projections",
    )(x, modkv, modb, gkv, gb, w_kv, w_qz)


def _suffix_weights():
    j = lax.broadcasted_iota(jnp.int32, (K_BLOCK, K_BLOCK), 0)
    s = lax.broadcasted_iota(jnp.int32, (K_BLOCK, K_BLOCK), 1)
    return jnp.where((j > s) | (s == K_BLOCK - 1), -1.0, 0.0).astype(BF16)


def _attn_kernel(q_ref, k_ref, v_ref, z_ref, w_ref, o_ref,
                 qm_ref, acc_ref, carry_ref, z2a_ref, z2b_ref, aa_ref, ab_ref):
    n_q = q_ref.shape[1] // Q_TILE
    n_groups = q_ref.shape[2] // LANES
    heads = tuple(range(2 * n_groups))
    lane = lax.broadcasted_iota(jnp.int32, (1, LANES), 1)
    head0 = lane < HEAD_DIM
    last_key = lax.broadcasted_iota(jnp.int32, (Q_TILE, K_BLOCK), 1) == K_BLOCK - 1
    quadrant = (Q_TILE // 2, K_BLOCK // 2)
    causal_q = (lax.broadcasted_iota(jnp.int32, quadrant, 1)
                < lax.broadcasted_iota(jnp.int32, quadrant, 0))
    w = w_ref[...]

    def block_start(kb):
        return pl.multiple_of(kb * K_BLOCK, K_BLOCK)

    def logits(h, kb, z2_buf):
        g = h // 2
        k_blk = k_ref[0, pl.ds(block_start(kb), K_BLOCK), g * LANES:(g + 1) * LANES]
        z2_buf[h] = lax.dot_general(qm_ref[h], k_blk, (((1,), (1,)), ((), ())),
                                    preferred_element_type=F32)

    def accumulate(h, a_buf, kb):
        g = h // 2
        v_blk = v_ref[0, pl.ds(block_start(kb), K_BLOCK), g * LANES:(g + 1) * LANES]
        acc_ref[h] += _dot(a_buf[h], v_blk)

    half = Q_TILE // 2

    def softplus2(z2):
        return jnp.maximum(z2, 0.0) + jnp.log(1.0 + jnp.exp2(-jnp.abs(z2))) * LOG2E

    def suffix_sums(h, z2_buf, masked):
        if not masked:
            z2 = z2_buf[h]
            sp = softplus2(z2)
            z2_buf[h] = z2 - sp
            return _dot(sp.astype(BF16), w)
        z_tl = z2_buf[h, :half, :half]
        sp_tl = softplus2(z_tl)
        z2_buf[h, :half, :half] = z_tl - sp_tl
        z_bot = z2_buf[h, half:, :]
        sp_bot = softplus2(z_bot)
        z2_buf[h, half:, :] = z_bot - sp_bot
        sp_top = jnp.concatenate([jnp.where(causal_q, sp_tl, 0.0), jnp.zeros_like(sp_tl)], axis=1)
        sp_bot = jnp.concatenate([sp_bot[:, :half], jnp.where(causal_q, sp_bot[:, half:], 0.0)],
                                 axis=1)
        return _dot(jnp.concatenate([sp_top, sp_bot], axis=0).astype(BF16), w)

    def weights(h, r, z2_buf, a_buf, masked):
        total = jnp.broadcast_to(r[:, K_BLOCK - 1:], (Q_TILE, LANES))
        if not masked:
            carry = carry_ref[h]
            nxt = carry + total
            carry_ref[h] = nxt
            suffix = jnp.where(last_key, 0.0, r)
            ah = jnp.exp2(z2_buf[h] + (suffix + jnp.concatenate([carry] * (K_BLOCK // LANES), axis=1)))
            a_buf[h] = ah.astype(BF16)
            return nxt
        carry_ref[h] = total
        a_tl = jnp.exp2(z2_buf[h, :half, :half] + r[:half, :half])
        a_bl = jnp.exp2(z2_buf[h, half:, :half] + r[half:, :half])
        a_br = jnp.exp2(z2_buf[h, half:, half:] + r[half:, half:])
        a_top = jnp.concatenate([jnp.where(causal_q, a_tl, 0.0), jnp.zeros_like(a_tl)], axis=1)
        a_bot = jnp.concatenate([a_bl, jnp.where(causal_q, a_br, 0.0)], axis=1)
        a_buf[h] = jnp.concatenate([a_top, a_bot], axis=0).astype(BF16)
        return total

    def block_step(kb, kb_prev, variant, masked):
        z2_cur, z2_next = (z2a_ref, z2b_ref) if variant == 0 else (z2b_ref, z2a_ref)
        a_cur, a_prev = (aa_ref, ab_ref) if variant == 0 else (ab_ref, aa_ref)
        if masked:
            for h in heads[:LOGITS_LEAD]:
                logits(h, kb, z2_cur)
        worst = None
        for i, h in enumerate(heads):
            if masked and i + LOGITS_LEAD < len(heads):
                logits(heads[i + LOGITS_LEAD], kb, z2_cur)
            r = suffix_sums(h, z2_cur, masked)
            logits(h, jnp.maximum(kb - 1, 0), z2_next)
            accumulate(h, a_prev, kb_prev)
            nxt = weights(h, r, z2_cur, a_cur, masked)
            if not masked:
                worst = nxt if worst is None else jnp.maximum(worst, nxt)
        return None if masked else jnp.max(worst)

    def alternate(variant, fn):
        return lax.cond(variant == 0, lambda: fn(0), lambda: fn(1))

    def finalize(qstart):
        for g in range(n_groups):
            lanes = slice(g * LANES, (g + 1) * LANES)
            zg = z_ref[0, pl.ds(qstart, Q_TILE), lanes]
            acc = jnp.where(head0, acc_ref[2 * g], acc_ref[2 * g + 1])
            o_ref[0, pl.ds(qstart, Q_TILE), lanes] = (acc * _silu(zg)).astype(o_ref.dtype)
        acc_ref[...] = jnp.zeros_like(acc_ref)

    def split_heads(qstart):
        for g in range(n_groups):
            q = q_ref[0, pl.ds(qstart, Q_TILE), g * LANES:(g + 1) * LANES]
            zero = jnp.zeros_like(q)
            qm_ref[2 * g] = jnp.where(head0, q, zero)
            qm_ref[2 * g + 1] = jnp.where(head0, zero, q)

    def q_tile(qi, state):
        variant, kb_prev = state

        def diagonal_step(v):
            split_heads(pl.multiple_of(qi * Q_TILE, Q_TILE))
            block_step(qi, kb_prev, v, True)
            finalize(pl.multiple_of((qi - 1) * Q_TILE, Q_TILE))

        alternate(variant, diagonal_step)

        def more_blocks(state):
            j, worst, _ = state
            return jnp.logical_and(j < qi, worst > NEGLIGIBLE_LOG2)

        def below_diagonal(state):
            j, _, variant = state
            kb = qi - 1 - j
            worst = alternate(variant, lambda v: block_step(kb, kb + 1, v, False))
            return j + 1, worst, 1 - variant

        n_below, _, variant = lax.while_loop(more_blocks, below_diagonal,
                                             (jnp.int32(0), jnp.float32(0.0), 1 - variant))
        return variant, qi - n_below

    first = jnp.int32(0)
    split_heads(0)
    for h in heads:
        logits(h, first, z2a_ref)
    for h in heads:
        weights(h, suffix_sums(h, z2a_ref, True), z2a_ref, aa_ref, True)
    acc_ref[...] = jnp.zeros_like(acc_ref)
    variant, kb_prev = lax.fori_loop(1, n_q, q_tile, (jnp.int32(1), first))

    def drain(v):
        a_prev = ab_ref if v == 0 else aa_ref
        for h in heads:
            accumulate(h, a_prev, kb_prev)

    alternate(variant, drain)
    finalize((n_q - 1) * Q_TILE)


def _attention(q, kv, z):
    bsz, s, d = q.shape
    width = HEAD_GROUPS_PER_STEP * LANES
    n_steps = d // width
    n_heads = 2 * HEAD_GROUPS_PER_STEP
    grp = lambda b, g: (b, 0, g)
    return pl.pallas_call(
        _attn_kernel,
        grid=(bsz, n_steps),
        in_specs=[
            pl.BlockSpec((1, s, width), grp),
            pl.BlockSpec((1, s, width), grp),
            pl.BlockSpec((1, s, width), lambda b, g: (b, 0, n_steps + g)),
            pl.BlockSpec((1, s, width), grp),
            pl.BlockSpec((K_BLOCK, K_BLOCK), lambda b, g: (0, 0)),
        ],
        out_specs=pl.BlockSpec((1, s, width), grp),
        out_shape=jax.ShapeDtypeStruct((bsz, s, d), BF16),
        scratch_shapes=[
            pltpu.VMEM((n_heads, Q_TILE, LANES), BF16),
            pltpu.VMEM((n_heads, Q_TILE, LANES), F32),
            pltpu.VMEM((n_heads, Q_TILE, LANES), F32),
            pltpu.VMEM((n_heads, Q_TILE, K_BLOCK), F32),
            pltpu.VMEM((n_heads, Q_TILE, K_BLOCK), F32),
            pltpu.VMEM((n_heads, Q_TILE, K_BLOCK), BF16),
            pltpu.VMEM((n_heads, Q_TILE, K_BLOCK), BF16),
        ],
        compiler_params=pltpu.CompilerParams(
            dimension_semantics=("arbitrary", "arbitrary"),
            vmem_limit_bytes=_vmem_limit(5 / 8)),
        name="stickbreak_attention",
    )(q, kv, kv, z, _suffix_weights())


def _out_kernel(o_hbm, x_hbm, mod_ref, wout_ref, g_ref, y_ref, obuf, xbuf, sem):
    n_i = pl.num_programs(1)
    n_steps = pl.num_programs(0) * n_i
    t = pl.program_id(0) * n_i + pl.program_id(1)
    tm = xbuf.shape[1]

    def copies(step):
        b = step // n_i
        r0 = pl.multiple_of((step % n_i) * tm, tm)
        slot = step % OUT_RING
        return (pltpu.make_async_copy(o_hbm.at[b, pl.ds(r0, tm), :], obuf.at[slot], sem.at[0, slot]),
                pltpu.make_async_copy(x_hbm.at[b, pl.ds(r0, tm), :], xbuf.at[slot], sem.at[1, slot]))

    def start(step):
        for cp in copies(step):
            cp.start()

    @pl.when(t == 0)
    def _():
        for step in range(OUT_RING - 1):
            @pl.when(step < n_steps)
            def _():
                start(step)

    @pl.when(t + (OUT_RING - 1) < n_steps)
    def _():
        start(t + (OUT_RING - 1))

    for cp in copies(t):
        cp.wait()

    slot = t % OUT_RING
    gate = mod_ref[0][2:3]
    w_out = wout_ref[...].astype(BF16)
    rows = tm // CONV_ROW_CHUNKS
    for r0 in range(0, tm, rows):
        x = xbuf[slot, r0:r0 + rows, :] + gate * _dot(obuf[slot, r0:r0 + rows, :], w_out)
        y_ref[0, r0:r0 + rows, :] = (
            x * lax.rsqrt(jnp.mean(x * x, axis=-1, keepdims=True) + EPS)) * g_ref[...]


def _out_layer(o, x, modb, w_out, g):
    bsz, s, d = x.shape
    tm = ROW_TILE
    row = lambda b, i: (b, i, 0)
    return pl.pallas_call(
        _out_kernel,
        grid=(bsz, s // tm),
        in_specs=[
            pl.BlockSpec(memory_space=pl.ANY),
            pl.BlockSpec(memory_space=pl.ANY),
            pl.BlockSpec((1, 3, d), lambda b, i: (b, 0, 0)),
            pl.BlockSpec((d, d), lambda b, i: (0, 0), pipeline_mode=pl.Buffered(1)),
            pl.BlockSpec((1, d), lambda b, i: (0, 0)),
        ],
        out_specs=pl.BlockSpec((1, tm, d), row),
        out_shape=jax.ShapeDtypeStruct((bsz, s, d), F32),
        scratch_shapes=[
            pltpu.VMEM((OUT_RING, tm, d), BF16),
            pltpu.VMEM((OUT_RING, tm, d), F32),
            pltpu.SemaphoreType.DMA((2, OUT_RING)),
        ],
        compiler_params=pltpu.CompilerParams(
            dimension_semantics=("arbitrary", "arbitrary"),
            vmem_limit_bytes=_vmem_limit(5 / 8)),
        name="out_layer",
    )(o, x, modb, w_out, g)


def kernel(x, c, a_mod_w, a_mod_b, a_norm_g, a_w_in, a_conv_w, a_w_out,
           kv_mod_w, kv_mod_b, kv_norm_g, w_kv,
           b_mod_w, b_mod_b, b_norm_g, b_w_qz, b_w_out, final_norm_g):
    bsz, _, d = x.shape
    assert d == D_MODEL and a_mod_w.shape[0] == 1 and b_mod_w.shape[0] == 1

    mod = _modulation(c, [a_mod_w.reshape(d, 3 * d), kv_mod_w, b_mod_w.reshape(d, 3 * d)],
                      [a_mod_b, kv_mod_b, b_mod_b])
    mod_a = mod[:, :3 * d].reshape(bsz, 3, d)
    mod_kv = mod[:, 3 * d:5 * d].reshape(bsz, 2, d)
    mod_b = mod[:, 5 * d:].reshape(bsz, 3, d)

    x1 = _conv_layer(x, mod_a, a_norm_g, a_w_in[0], a_conv_w[0], a_w_out[0])
    q, kv, z = _projections(x1, mod_kv, mod_b, kv_norm_g.reshape(1, d), b_norm_g,
                            w_kv, b_w_qz[0])
    o = _attention(q, kv, z)
    return _out_layer(o, x1, mod_b, b_w_out[0], final_norm_g.reshape(1, d))
```

```python
import functools
import math

import jax
import jax.numpy as jnp
from jax import lax
from jax.experimental import pallas as pl
from jax.experimental.pallas import tpu as pltpu

D_MODEL = 1024
N_HEADS = 16
HEAD_DIM = D_MODEL // N_HEADS
CONV_K = 3
EPS = 1e-6
LOG2E = 1.4426950408889634

LANES = 128
SUBLANES = 8
MXU_DIM = 256

VMEM_BYTES = 64 * 1024 * 1024


def _vmem_limit(fraction):
    return int(VMEM_BYTES * fraction)


ROW_TILE = 1024
MOD_COL_TILE = 1024
HEAD_GROUPS_PER_STEP = 4

K_BLOCK = MXU_DIM
Q_TILE = K_BLOCK
NEGLIGIBLE_LOG2 = -135.0
CONV_ROW_CHUNKS = 4
OUT_RING = 3
LOGITS_LEAD = 2

BF16 = jnp.bfloat16
F32 = jnp.float32


def _dot(a, b):
    return jnp.dot(a, b, preferred_element_type=F32)


def _silu(x):
    return x * jax.nn.sigmoid(x)


def _mod_kernel(c_ref, *refs, starts):
    n = len(starts) - 1
    w_refs, b_refs, o_ref = refs[:n], refs[n:2 * n], refs[2 * n]
    j = pl.program_id(0)
    s = _silu(c_ref[...]).astype(BF16)
    for i in range(n):
        @pl.when(jnp.logical_and(j >= starts[i], j < starts[i + 1]))
        def _(i=i):
            o_ref[...] = _dot(s, w_refs[i][...].astype(BF16)) + b_refs[i][...]


def _modulation(c, ws, bs):
    bsz, d = c.shape
    tn = MOD_COL_TILE
    tiles = [w.shape[1] // tn for w in ws]
    starts = [sum(tiles[:i]) for i in range(len(ws) + 1)]

    def tile_of(i):
        return lambda j: (0, jnp.clip(j - starts[i], 0, tiles[i] - 1))

    return pl.pallas_call(
        functools.partial(_mod_kernel, starts=tuple(starts)),
        grid=(starts[-1],),
        in_specs=([pl.BlockSpec((bsz, d), lambda j: (0, 0))]
                  + [pl.BlockSpec((d, tn), tile_of(i)) for i in range(len(ws))]
                  + [pl.BlockSpec((1, tn), tile_of(i)) for i in range(len(ws))]),
        out_specs=pl.BlockSpec((bsz, tn), lambda j: (0, j)),
        out_shape=jax.ShapeDtypeStruct((bsz, starts[-1] * tn), F32),
        compiler_params=pltpu.CompilerParams(
            dimension_semantics=("arbitrary",),
            vmem_limit_bytes=_vmem_limit(5 / 8)),
        name="modulation",
    )(c, *ws, *[b.reshape(1, -1) for b in bs])


def _conv_layer_kernel(x_ref, mod_ref, g_ref, win_hbm, cw_ref, wout_hbm, o_ref,
                       carry_ref, win_ref, wout_ref, sem):
    tm = x_ref.shape[1]
    d = x_ref.shape[2]

    @pl.when(pl.program_id(1) == 0)
    def _():
        carry_ref[...] = jnp.zeros_like(carry_ref)

    def weight_copies():
        cols = lambda p: pl.ds(p * d, d)
        return ([pltpu.make_async_copy(win_hbm.at[:, cols(p)], win_ref.at[:, cols(p)], sem.at[p])
                 for p in range(4)]
                + [pltpu.make_async_copy(wout_hbm, wout_ref, sem.at[4])])

    def body(first_step):
        mod = mod_ref[0]
        shift, scale, gate = mod[0:1], mod[1:2], mod[2:3]
        gain = g_ref[...] * (1.0 + scale)
        cw = cw_ref[...]
        copies = weight_copies()
        if first_step:
            for cp in copies:
                cp.start()
        cast = {}

        def weight(p):
            if p not in cast:
                if first_step:
                    copies[p].wait()
                cast[p] = (win_ref[:, p * d:(p + 1) * d] if p < 4 else wout_ref[...]).astype(BF16)
            return cast[p]

        if not first_step:
            for p in range(5):
                weight(p)

        rows = tm // CONV_ROW_CHUNKS
        tail = carry_ref[...]
        for r0 in range(0, tm, rows):
            x = x_ref[0, r0:r0 + rows, :]
            rs = lax.rsqrt(jnp.mean(x * x, axis=-1, keepdims=True) + EPS)
            h = ((x * rs) * gain + shift).astype(BF16)
            b_gate, c_gate, u, z = [_dot(h, weight(p)) for p in range(4)]

            cu = c_gate * u
            ext = jnp.concatenate([tail, cu], axis=0)
            prev1 = pltpu.roll(ext, 1, 0)[SUBLANES:]
            prev2 = pltpu.roll(ext, 2, 0)[SUBLANES:]
            tail = cu[rows - SUBLANES:]
            conv = cw[0:1] * prev2 + cw[1:2] * prev1 + cw[2:3] * cu

            y = (b_gate * conv) * _silu(z)
            o_ref[0, r0:r0 + rows, :] = x + gate * _dot(y.astype(BF16), weight(4))
        carry_ref[...] = tail

    first = (pl.program_id(0) == 0) & (pl.program_id(1) == 0)
    lax.cond(first, lambda: body(True), lambda: body(False))


def _conv_layer(x, mod, g, w_in, conv_w, w_out):
    bsz, s, d = x.shape
    tm = ROW_TILE
    const = dict(pipeline_mode=pl.Buffered(1))
    return pl.pallas_call(
        _conv_layer_kernel,
        grid=(bsz, s // tm),
        in_specs=[
            pl.BlockSpec((1, tm, d), lambda b, i: (b, i, 0)),
            pl.BlockSpec((1, 3, d), lambda b, i: (b, 0, 0)),
            pl.BlockSpec((1, d), lambda b, i: (0, 0)),
            pl.BlockSpec(memory_space=pl.ANY),
            pl.BlockSpec((CONV_K, d), lambda b, i: (0, 0)),
            pl.BlockSpec(memory_space=pl.ANY),
        ],
        out_specs=pl.BlockSpec((1, tm, d), lambda b, i: (b, i, 0)),
        out_shape=jax.ShapeDtypeStruct((bsz, s, d), F32),
        scratch_shapes=[
            pltpu.VMEM((SUBLANES, d), F32),
            pltpu.VMEM((d, 4 * d), F32),
            pltpu.VMEM((d, d), F32),
            pltpu.SemaphoreType.DMA((5,)),
        ],
        compiler_params=pltpu.CompilerParams(
            dimension_semantics=("arbitrary", "arbitrary"),
            vmem_limit_bytes=_vmem_limit(7 / 8)),
        name="conv_layer",
    )(x, mod, g, w_in, conv_w, w_out)


def _proj_kernel(x_ref, modkv_ref, modb_ref, gkv_ref, gb_ref, wkv_ref, wqz_ref,
                 q_ref, kv_ref, z_ref):
    d = x_ref.shape[2]
    x = x_ref[0]
    xn = x * lax.rsqrt(jnp.mean(x * x, axis=-1, keepdims=True) + EPS)
    modkv = modkv_ref[0]
    modb = modb_ref[0]
    hkv = (xn * (gkv_ref[...] * (1.0 + modkv[1:2])) + modkv[0:1]).astype(BF16)
    hq = (xn * (gb_ref[...] * (1.0 + modb[1:2])) + modb[0:1]).astype(BF16)
    kv_ref[0] = _dot(hkv, wkv_ref[...].astype(BF16)).astype(BF16)
    q_ref[0] = (_dot(hq, wqz_ref[:, :d].astype(BF16)) * (LOG2E / math.sqrt(HEAD_DIM))).astype(BF16)
    z_ref[0] = _dot(hq, wqz_ref[:, d:].astype(BF16))


def _projections(x, modkv, modb, gkv, gb, w_kv, w_qz):
    bsz, s, d = x.shape
    tm = ROW_TILE
    const = dict(pipeline_mode=pl.Buffered(1))
    row = lambda b, i: (b, i, 0)
    return pl.pallas_call(
        _proj_kernel,
        grid=(bsz, s // tm),
        in_specs=[
            pl.BlockSpec((1, tm, d), row),
            pl.BlockSpec((1, 2, d), lambda b, i: (b, 0, 0)),
            pl.BlockSpec((1, 3, d), lambda b, i: (b, 0, 0)),
            pl.BlockSpec((1, d), lambda b, i: (0, 0)),
            pl.BlockSpec((1, d), lambda b, i: (0, 0)),
            pl.BlockSpec((d, 2 * d), lambda b, i: (0, 0), **const),
            pl.BlockSpec((d, 2 * d), lambda b, i: (0, 0), **const),
        ],
        out_specs=[
            pl.BlockSpec((1, tm, d), row),
            pl.BlockSpec((1, tm, 2 * d), row),
            pl.BlockSpec((1, tm, d), row),
        ],
        out_shape=[
            jax.ShapeDtypeStruct((bsz, s, d), BF16),
            jax.ShapeDtypeStruct((bsz, s, 2 * d), BF16),
            jax.ShapeDtypeStruct((bsz, s, d), F32),
        ],
        compiler_params=pltpu.CompilerParams(
            dimension_semantics=("arbitrary", "arbitrary"),
            vmem_limit_bytes=_vmem_limit(7 / 8)),
        name="projections",
    )(x, modkv, modb, gkv, gb, w_kv, w_qz)


def _suffix_weights():
    j = lax.broadcasted_iota(jnp.int32, (K_BLOCK, K_BLOCK), 0)
    s = lax.broadcasted_iota(jnp.int32, (K_BLOCK, K_BLOCK), 1)
    return jnp.where((j > s) | (s == K_BLOCK - 1), -1.0, 0.0).astype(BF16)


def _attn_kernel(q_ref, k_ref, v_ref, z_ref, w_ref, o_ref,
                 qm_ref, acc_ref, carry_ref, z2a_ref, z2b_ref, aa_ref, ab_ref):
    n_q = q_ref.shape[1] // Q_TILE
    n_groups = q_ref.shape[2] // LANES
    heads = tuple(range(2 * n_groups))
    lane = lax.broadcasted_iota(jnp.int32, (1, LANES), 1)
    head0 = lane < HEAD_DIM
    last_key = lax.broadcasted_iota(jnp.int32, (Q_TILE, K_BLOCK), 1) == K_BLOCK - 1
    quadrant = (Q_TILE // 2, K_BLOCK // 2)
    causal_q = (lax.broadcasted_iota(jnp.int32, quadrant, 1)
                < lax.broadcasted_iota(jnp.int32, quadrant, 0))
    w = w_ref[...]

    def block_start(kb):
        return pl.multiple_of(kb * K_BLOCK, K_BLOCK)

    def logits(h, kb, z2_buf):
        g = h // 2
        k_blk = k_ref[0, pl.ds(block_start(kb), K_BLOCK), g * LANES:(g + 1) * LANES]
        z2_buf[h] = lax.dot_general(qm_ref[h], k_blk, (((1,), (1,)), ((), ())),
                                    preferred_element_type=F32)

    def accumulate(h, a_buf, kb):
        g = h // 2
        v_blk = v_ref[0, pl.ds(block_start(kb), K_BLOCK), g * LANES:(g + 1) * LANES]
        acc_ref[h] += _dot(a_buf[h], v_blk)

    half = Q_TILE // 2

    def softplus2(z2):
        return jnp.maximum(z2, 0.0) + jnp.log(1.0 + jnp.exp2(-jnp.abs(z2))) * LOG2E

    def suffix_sums(h, z2_buf, masked):
        if not masked:
            z2 = z2_buf[h]
            sp = softplus2(z2)
            z2_buf[h] = z2 - sp
            return _dot(sp.astype(BF16), w)
        z_tl = z2_buf[h, :half, :half]
        sp_tl = softplus2(z_tl)
        z2_buf[h, :half, :half] = z_tl - sp_tl
        z_bot = z2_buf[h, half:, :]
        sp_bot = softplus2(z_bot)
        z2_buf[h, half:, :] = z_bot - sp_bot
        sp_top = jnp.concatenate([jnp.where(causal_q, sp_tl, 0.0), jnp.zeros_like(sp_tl)], axis=1)
        sp_bot = jnp.concatenate([sp_bot[:, :half], jnp.where(causal_q, sp_bot[:, half:], 0.0)],
                                 axis=1)
        return _dot(jnp.concatenate([sp_top, sp_bot], axis=0).astype(BF16), w)

    def weights(h, r, z2_buf, a_buf, masked):
        total = jnp.broadcast_to(r[:, K_BLOCK - 1:], (Q_TILE, LANES))
        if not masked:
            carry = carry_ref[h]
            nxt = carry + total
            carry_ref[h] = nxt
            suffix = jnp.where(last_key, 0.0, r)
            ah = jnp.exp2(z2_buf[h] + (suffix + jnp.concatenate([carry] * (K_BLOCK // LANES), axis=1)))
            a_buf[h] = ah.astype(BF16)
            return nxt
        carry_ref[h] = total
        a_tl = jnp.exp2(z2_buf[h, :half, :half] + r[:half, :half])
        a_bl = jnp.exp2(z2_buf[h, half:, :half] + r[half:, :half])
        a_br = jnp.exp2(z2_buf[h, half:, half:] + r[half:, half:])
        a_top = jnp.concatenate([jnp.where(causal_q, a_tl, 0.0), jnp.zeros_like(a_tl)], axis=1)
        a_bot = jnp.concatenate([a_bl, jnp.where(causal_q, a_br, 0.0)], axis=1)
        a_buf[h] = jnp.concatenate([a_top, a_bot], axis=0).astype(BF16)
        return total

    def block_step(kb, kb_prev, variant, masked):
        z2_cur, z2_next = (z2a_ref, z2b_ref) if variant == 0 else (z2b_ref, z2a_ref)
        a_cur, a_prev = (aa_ref, ab_ref) if variant == 0 else (ab_ref, aa_ref)
        if masked:
            for h in heads[:LOGITS_LEAD]:
                logits(h, kb, z2_cur)
        worst = None
        for i, h in enumerate(heads):
            if masked and i + LOGITS_LEAD < len(heads):
                logits(heads[i + LOGITS_LEAD], kb, z2_cur)
            r = suffix_sums(h, z2_cur, masked)
            logits(h, jnp.maximum(kb - 1, 0), z2_next)
            accumulate(h, a_prev, kb_prev)
            nxt = weights(h, r, z2_cur, a_cur, masked)
            if not masked:
                worst = nxt if worst is None else jnp.maximum(worst, nxt)
        return None if masked else jnp.max(worst)

    def alternate(variant, fn):
        return lax.cond(variant == 0, lambda: fn(0), lambda: fn(1))

    def finalize(qstart):
        for g in range(n_groups):
            lanes = slice(g * LANES, (g + 1) * LANES)
            zg = z_ref[0, pl.ds(qstart, Q_TILE), lanes]
            acc = jnp.where(head0, acc_ref[2 * g], acc_ref[2 * g + 1])
            o_ref[0, pl.ds(qstart, Q_TILE), lanes] = (acc * _silu(zg)).astype(o_ref.dtype)
        acc_ref[...] = jnp.zeros_like(acc_ref)

    def split_heads(qstart):
        for g in range(n_groups):
            q = q_ref[0, pl.ds(qstart, Q_TILE), g * LANES:(g + 1) * LANES]
            zero = jnp.zeros_like(q)
            qm_ref[2 * g] = jnp.where(head0, q, zero)
            qm_ref[2 * g + 1] = jnp.where(head0, zero, q)

    def q_tile(qi, state):
        variant, kb_prev = state

        def diagonal_step(v):
            split_heads(pl.multiple_of(qi * Q_TILE, Q_TILE))
            block_step(qi, kb_prev, v, True)
            finalize(pl.multiple_of((qi - 1) * Q_TILE, Q_TILE))

        alternate(variant, diagonal_step)

        def more_blocks(state):
            j, worst, _ = state
            return jnp.logical_and(j < qi, worst > NEGLIGIBLE_LOG2)

        def below_diagonal(state):
            j, _, variant = state
            kb = qi - 1 - j
            worst = alternate(variant, lambda v: block_step(kb, kb + 1, v, False))
            return j + 1, worst, 1 - variant

        n_below, _, variant = lax.while_loop(more_blocks, below_diagonal,
                                             (jnp.int32(0), jnp.float32(0.0), 1 - variant))
        return variant, qi - n_below

    first = jnp.int32(0)
    split_heads(0)
    for h in heads:
        logits(h, first, z2a_ref)
    for h in heads:
        weights(h, suffix_sums(h, z2a_ref, True), z2a_ref, aa_ref, True)
    acc_ref[...] = jnp.zeros_like(acc_ref)
    variant, kb_prev = lax.fori_loop(1, n_q, q_tile, (jnp.int32(1), first))

    def drain(v):
        a_prev = ab_ref if v == 0 else aa_ref
        for h in heads:
            accumulate(h, a_prev, kb_prev)

    alternate(variant, drain)
    finalize((n_q - 1) * Q_TILE)


def _attention(q, kv, z):
    bsz, s, d = q.shape
    width = HEAD_GROUPS_PER_STEP * LANES
    n_steps = d // width
    n_heads = 2 * HEAD_GROUPS_PER_STEP
    grp = lambda b, g: (b, 0, g)
    return pl.pallas_call(
        _attn_kernel,
        grid=(bsz, n_steps),
        in_specs=[
            pl.BlockSpec((1, s, width), grp),
            pl.BlockSpec((1, s, width), grp),
            pl.BlockSpec((1, s, width), lambda b, g: (b, 0, n_steps + g)),
            pl.BlockSpec((1, s, width), grp),
            pl.BlockSpec((K_BLOCK, K_BLOCK), lambda b, g: (0, 0)),
        ],
        out_specs=pl.BlockSpec((1, s, width), grp),
        out_shape=jax.ShapeDtypeStruct((bsz, s, d), BF16),
        scratch_shapes=[
            pltpu.VMEM((n_heads, Q_TILE, LANES), BF16),
            pltpu.VMEM((n_heads, Q_TILE, LANES), F32),
            pltpu.VMEM((n_heads, Q_TILE, LANES), F32),
            pltpu.VMEM((n_heads, Q_TILE, K_BLOCK), F32),
            pltpu.VMEM((n_heads, Q_TILE, K_BLOCK), F32),
            pltpu.VMEM((n_heads, Q_TILE, K_BLOCK), BF16),
            pltpu.VMEM((n_heads, Q_TILE, K_BLOCK), BF16),
        ],
        compiler_params=pltpu.CompilerParams(
            dimension_semantics=("arbitrary", "arbitrary"),
            vmem_limit_bytes=_vmem_limit(5 / 8)),
        name="stickbreak_attention",
    )(q, kv, kv, z, _suffix_weights())


def _out_kernel(o_hbm, x_hbm, mod_ref, wout_ref, g_ref, y_ref, obuf, xbuf, sem):
    n_i = pl.num_programs(1)
    n_steps = pl.num_programs(0) * n_i
    t = pl.program_id(0) * n_i + pl.program_id(1)
    tm = xbuf.shape[1]

    def copies(step):
        b = step // n_i
        r0 = pl.multiple_of((step % n_i) * tm, tm)
        slot = step % OUT_RING
        return (pltpu.make_async_copy(o_hbm.at[b, pl.ds(r0, tm), :], obuf.at[slot], sem.at[0, slot]),
                pltpu.make_async_copy(x_hbm.at[b, pl.ds(r0, tm), :], xbuf.at[slot], sem.at[1, slot]))

    def start(step):
        for cp in copies(step):
            cp.start()

    @pl.when(t == 0)
    def _():
        for step in range(OUT_RING - 1):
            @pl.when(step < n_steps)
            def _():
                start(step)

    @pl.when(t + (OUT_RING - 1) < n_steps)
    def _():
        start(t + (OUT_RING - 1))

    for cp in copies(t):
        cp.wait()

    slot = t % OUT_RING
    gate = mod_ref[0][2:3]
    w_out = wout_ref[...].astype(BF16)
    rows = tm // CONV_ROW_CHUNKS
    for r0 in range(0, tm, rows):
        x = xbuf[slot, r0:r0 + rows, :] + gate * _dot(obuf[slot, r0:r0 + rows, :], w_out)
        y_ref[0, r0:r0 + rows, :] = (
            x * lax.rsqrt(jnp.mean(x * x, axis=-1, keepdims=True) + EPS)) * g_ref[...]


def _out_layer(o, x, modb, w_out, g):
    bsz, s, d = x.shape
    tm = ROW_TILE
    row = lambda b, i: (b, i, 0)
    return pl.pallas_call(
        _out_kernel,
        grid=(bsz, s // tm),
        in_specs=[
            pl.BlockSpec(memory_space=pl.ANY),
            pl.BlockSpec(memory_space=pl.ANY),
            pl.BlockSpec((1, 3, d), lambda b, i: (b, 0, 0)),
            pl.BlockSpec((d, d), lambda b, i: (0, 0), pipeline_mode=pl.Buffered(1)),
            pl.BlockSpec((1, d), lambda b, i: (0, 0)),
        ],
        out_specs=pl.BlockSpec((1, tm, d), row),
        out_shape=jax.ShapeDtypeStruct((bsz, s, d), F32),
        scratch_shapes=[
            pltpu.VMEM((OUT_RING, tm, d), BF16),
            pltpu.VMEM((OUT_RING, tm, d), F32),
            pltpu.SemaphoreType.DMA((2, OUT_RING)),
        ],
        compiler_params=pltpu.CompilerParams(
            dimension_semantics=("arbitrary", "arbitrary"),
            vmem_limit_bytes=_vmem_limit(5 / 8)),
        name="out_layer",
    )(o, x, modb, w_out, g)


def kernel(x, c, a_mod_w, a_mod_b, a_norm_g, a_w_in, a_conv_w, a_w_out,
           kv_mod_w, kv_mod_b, kv_norm_g, w_kv,
           b_mod_w, b_mod_b, b_norm_g, b_w_qz, b_w_out, final_norm_g):
    bsz, _, d = x.shape
    assert d == D_MODEL and a_mod_w.shape[0] == 1 and b_mod_w.shape[0] == 1

    mod = _modulation(c, [a_mod_w.reshape(d, 3 * d), kv_mod_w, b_mod_w.reshape(d, 3 * d)],
                      [a_mod_b, kv_mod_b, b_mod_b])
    mod_a = mod[:, :3 * d].reshape(bsz, 3, d)
    mod_kv = mod[:, 3 * d:5 * d].reshape(bsz, 2, d)
    mod_b = mod[:, 5 * d:].reshape(bsz, 3, d)

    x1 = _conv_layer(x, mod_a, a_norm_g, a_w_in[0], a_conv_w[0], a_w_out[0])
    q, kv, z = _projections(x1, mod_kv, mod_b, kv_norm_g.reshape(1, d), b_norm_g,
                            w_kv, b_w_qz[0])
    o = _attention(q, kv, z)
    return _out_layer(o, x1, mod_b, b_w_out[0], final_norm_g.reshape(1, d))
```
